```python
import math
import jax
import jax.numpy as jnp
from jax import lax
import numpy as np


D_MODEL = 1024
BATCH = 2
SEQ = 16384
DEPTH = 4
DEC_BATCH = 32
DEC_SEQ = 2048
PAST_LEN = 128

GRID_W = 64
RET_H = 4
RET_DK = D_MODEL // 8
RET_DV = D_MODEL // 8
RET_CHUNK = 128
ROPE_BASE = 10000.0
DN_H = 4
DN_DK = D_MODEL // 8
DN_DV = D_MODEL // 8
DN_CONV = 3
DN_CHUNK = 64
NA_H = 8
NA_DH = D_MODEL // 16
NA_WIN_R = 8
NA_WIN_C = 16
N_BRANCH = 3
BRANCH_W = D_MODEL // 2
EC_EXPERTS = 16
EC_CAPACITY = 2
EC_FF = D_MODEL
NORM_EPS = 1e-6
GN_EPS = 1e-5
SPLIT_WIDTHS = (
    RET_H * RET_DK, RET_H * RET_DK, RET_H * RET_DV, RET_H * RET_DV,
    2 * DN_H * DN_DK + DN_H * DN_DV, DN_H * DN_DV, 2 * DN_H, 2 * DN_H,
    NA_H * NA_DH, NA_H * NA_DH, NA_H * NA_DH,
    N_BRANCH * D_MODEL,
)
IN_WIDTH = sum(SPLIT_WIDTHS)

kernel_name = 'hybrid_bidir_retention_gdn_natten_ec'


def rms_norm(x, w, eps=NORM_EPS):
    xf = x.astype(jnp.float32)
    y = xf * lax.rsqrt(jnp.mean(xf * xf, axis=-1, keepdims=True) + eps)
    return (y * w.astype(jnp.float32)).astype(x.dtype)


def _split_cols(p):
    out, off = [], 0
    for w in SPLIT_WIDTHS:
        out.append(p[..., off:off + w])
        off += w
    return out


def _l2norm(x):
    return x * lax.rsqrt(jnp.sum(x * x, axis=-1, keepdims=True) + 1e-6)


def rotary(x, pos):
    half = x.shape[-1] // 2
    inv = ROPE_BASE ** (-jnp.arange(half, dtype=jnp.float32) / half)
    ang = pos[:, None] * inv[None, :]
    cos = jnp.cos(ang)[None, :, None, :]
    sin = jnp.sin(ang)[None, :, None, :]
    x1, x2 = x[..., :half], x[..., half:]
    return jnp.concatenate([x1 * cos - x2 * sin, x1 * sin + x2 * cos], axis=-1)


def _retention_scan(q, k, v, log_gamma, include_diag):
    B, L, H, dk = q.shape
    dv = v.shape[-1]
    C = RET_CHUNK
    N = L // C
    q = q.reshape(B, N, C, H, dk)
    k = k.reshape(B, N, C, H, dk)
    v = v.reshape(B, N, C, H, dv)
    idx = jnp.arange(C, dtype=jnp.float32)
    diff = idx[:, None] - idx[None, :]
    mask = (diff >= 0) if include_diag else (diff > 0)
    dmask = jnp.where(mask, jnp.exp(log_gamma[:, None, None] * jnp.maximum(diff, 0.0)), 0.0)
    s = jnp.einsum('bnihd,bnjhd->bnhij', q, k) * dmask
    intra = jnp.einsum('bnhij,bnjhe->bnihe', s, v)
    k_dec = k * jnp.exp(log_gamma[None, :] * (C - 1.0 - idx)[:, None])[:, :, None]
    kv = jnp.einsum('bnjhd,bnjhe->nbhde', k_dec, v)
    chunk_decay = jnp.exp(log_gamma * C)[:, None, None]

    def step(S, kv_n):
        return S * chunk_decay + kv_n, S

    _, S_prev = lax.scan(step, jnp.zeros((B, H, dk, dv), jnp.float32), kv)
    q_dec = q * jnp.exp(log_gamma[None, :] * (idx + 1.0)[:, None])[:, :, None]
    inter = jnp.einsum('bnihd,nbhde->bnihe', q_dec, S_prev)
    return (intra + inter).reshape(B, L, H, dv)


def retention_branch(q, k, v, g, decay_logit, gn_w):
    B, L, _ = q.shape
    f32 = jnp.float32
    pos = jnp.arange(L, dtype=f32)
    q = rotary(q.astype(f32).reshape(B, L, RET_H, RET_DK), pos)
    k = rotary(k.astype(f32).reshape(B, L, RET_H, RET_DK), pos) * RET_DK ** -0.5
    v = v.astype(f32).reshape(B, L, RET_H, RET_DV)
    lg = jax.nn.log_sigmoid(decay_logit.astype(f32))
    fwd = _retention_scan(q, k, v, lg[0], True)
    bwd = jnp.flip(_retention_scan(jnp.flip(q, 1), jnp.flip(k, 1), jnp.flip(v, 1), lg[1], False), 1)
    r = fwd + bwd
    mu = jnp.mean(r, axis=-1, keepdims=True)
    var = jnp.mean(jnp.square(r - mu), axis=-1, keepdims=True)
    r = ((r - mu) * lax.rsqrt(var + GN_EPS)).reshape(B, L, RET_H * RET_DV) * gn_w.astype(f32)
    return (jax.nn.silu(g.astype(f32)) * r).astype(g.dtype)


def _gated_delta_scan(q, k, v, g, beta):
    B, L, H, dk = q.shape
    dv = v.shape[-1]
    C = DN_CHUNK
    N = L // C

    def chunks(x):
        return x.reshape(B, N, C, H, -1).transpose(0, 1, 3, 2, 4)

    q, k, v = chunks(q), chunks(k), chunks(v)
    g = g.reshape(B, N, C, H).transpose(0, 1, 3, 2)
    beta = beta.reshape(B, N, C, H).transpose(0, 1, 3, 2)
    gc = jnp.cumsum(g, axis=-1)
    idx = jnp.arange(C)
    incl = idx[:, None] >= idx[None, :]
    strict = idx[:, None] > idx[None, :]
    decay = jnp.exp(jnp.where(incl, gc[..., :, None] - gc[..., None, :], -jnp.inf))
    kb = k * beta[..., None]
    a = jnp.where(strict, jnp.einsum('bnhid,bnhjd->bnhij', kb, k) * decay, 0.0) + jnp.eye(C, dtype=q.dtype)
    rhs = jnp.concatenate([v * beta[..., None], kb * jnp.exp(gc)[..., None]], axis=-1)
    sol = lax.linalg.triangular_solve(a, rhs, left_side=True, lower=True)
    u, w = sol[..., :dv], sol[..., dv:]
    attn = jnp.einsum('bnhid,bnhjd->bnhij', q, k) * decay
    q_dec = q * jnp.exp(gc)[..., None]
    k_dec = k * jnp.exp(gc[..., -1:] - gc)[..., None]
    cdec = jnp.exp(gc[..., -1])
    xs = (jnp.moveaxis(w, 1, 0), jnp.moveaxis(u, 1, 0), jnp.moveaxis(q_dec, 1, 0),
          jnp.moveaxis(k_dec, 1, 0), jnp.moveaxis(attn, 1, 0), jnp.moveaxis(cdec, 1, 0))

    def step(S, xs_n):
        w_n, u_n, qd_n, kd_n, at_n, cd_n = xs_n
        v_new = u_n - jnp.einsum('bhck,bhkv->bhcv', w_n, S)
        o = jnp.einsum('bhck,bhkv->bhcv', qd_n, S) + jnp.einsum('bhij,bhjv->bhiv', at_n, v_new)
        S = S * cd_n[..., None, None] + jnp.einsum('bhck,bhcv->bhkv', kd_n, v_new)
        return S, o

    _, o = lax.scan(step, jnp.zeros((B, H, dk, dv), jnp.float32), xs)
    return o.transpose(1, 0, 3, 2, 4).reshape(B, L, H, dv)


def gated_deltanet_branch(qkv, gate, beta_logits, alpha_logits, conv_w, A_log, dt_bias, norm_w):
    B, L, Cq = qkv.shape
    f32 = jnp.float32
    y = lax.conv_general_dilated(qkv.astype(f32), conv_w.astype(f32)[:, None, :], window_strides=(1,),
                                 padding=[(DN_CONV // 2, DN_CONV // 2)],
                                 dimension_numbers=('NWC', 'WIO', 'NWC'), feature_group_count=Cq)
    y = jax.nn.silu(y)
    nqk = DN_H * DN_DK
    q = _l2norm(y[..., :nqk].reshape(B, L, DN_H, DN_DK)) * DN_DK ** -0.5
    k = _l2norm(y[..., nqk:2 * nqk].reshape(B, L, DN_H, DN_DK))
    v = y[..., 2 * nqk:].reshape(B, L, DN_H, DN_DV)
    beta = jax.nn.sigmoid(beta_logits.astype(f32)).reshape(B, L, 2, DN_H)
    g = -jnp.exp(A_log.astype(f32)) * jax.nn.softplus(alpha_logits.astype(f32).reshape(B, L, 2, DN_H) + dt_bias.astype(f32))
    o_f = _gated_delta_scan(q, k, v, g[:, :, 0], beta[:, :, 0])
    o_b = jnp.flip(_gated_delta_scan(jnp.flip(q, 1), jnp.flip(k, 1), jnp.flip(v, 1),
                                     jnp.flip(g[:, :, 1], 1), jnp.flip(beta[:, :, 1], 1)), 1)
    o = o_f + o_b
    o = o * lax.rsqrt(jnp.mean(o * o, axis=-1, keepdims=True) + NORM_EPS) * norm_w.astype(f32)
    o = o * jax.nn.silu(gate.astype(f32).reshape(B, L, DN_H, DN_DV))
    return o.reshape(B, L, DN_H * DN_DV).astype(qkv.dtype)


def neighborhood_attention_branch(q, k, v, rpb):
    B, L, _ = q.shape
    f32 = jnp.float32
    rows = L // GRID_W
    wr = min(NA_WIN_R, rows)
    n_cb = GRID_W // NA_WIN_C
    kspan = 2 * NA_WIN_C
    q = q.astype(f32).reshape(B, rows, GRID_W, NA_H, NA_DH) * NA_DH ** -0.5
    k = k.astype(f32).reshape(B, rows, GRID_W, NA_H, NA_DH)
    v = v.astype(f32).reshape(B, rows, GRID_W, NA_H, NA_DH)
    col = jnp.arange(GRID_W)
    qcol = col.reshape(n_cb, NA_WIN_C)
    col_start = jnp.clip(qcol - NA_WIN_C // 2, 0, GRID_W - NA_WIN_C)
    blk_start = jnp.clip(jnp.arange(n_cb) * NA_WIN_C - NA_WIN_C // 2, 0, GRID_W - kspan)
    key_cols = blk_start[:, None] + jnp.arange(kspan)
    kc = key_cols[:, None, :]
    col_ok = (kc >= col_start[..., None]) & (kc < col_start[..., None] + NA_WIN_C)
    dc = jnp.clip(kc - qcol[..., None], -(NA_WIN_C - 1), NA_WIN_C - 1) + NA_WIN_C - 1
    rpb_c = rpb.astype(f32)[:, :, dc]

    def row_block(r):
        rs = jnp.clip(r - wr // 2, 0, rows - wr)
        dr = rs + jnp.arange(wr) - r + NA_WIN_R - 1
        bias = jnp.where(col_ok[None, None], rpb_c[:, dr], -jnp.inf)
        kb = lax.dynamic_slice_in_dim(k, rs, wr, axis=1)[:, :, key_cols]
        vb = lax.dynamic_slice_in_dim(v, rs, wr, axis=1)[:, :, key_cols]
        qr = lax.dynamic_index_in_dim(q, r, axis=1, keepdims=False).reshape(B, n_cb, NA_WIN_C, NA_H, NA_DH)
        s = jnp.einsum('bjqhd,brjkhd->bhjqrk', qr, kb) + bias.transpose(0, 2, 3, 1, 4)[None]
        p = jax.nn.softmax(s.reshape(B, NA_H, n_cb, NA_WIN_C, wr * kspan), axis=-1).reshape(s.shape)
        o = jnp.einsum('bhjqrk,brjkhd->bjqhd', p, vb)
        return o.reshape(B, GRID_W, NA_H * NA_DH)

    out = lax.map(row_block, jnp.arange(rows))
    return out.transpose(1, 0, 2, 3).reshape(B, L, NA_H * NA_DH).astype(rpb.dtype)


def mixer_block(u, w_in, gate_b, ret_decay, ret_gn_w, dn_conv_w, dn_A_log, dn_dt_bias, dn_norm_w,
                na_rpb, w_branch, w_out):
    D = u.shape[-1]
    p = u @ w_in
    rq, rk, rv, rg, dqkv, dg, db, da, nq, nk, nv, mg = _split_cols(p)
    outs = (
        retention_branch(rq, rk, rv, rg, ret_decay, ret_gn_w),
        gated_deltanet_branch(dqkv, dg, db, da, dn_conv_w, dn_A_log, dn_dt_bias, dn_norm_w),
        neighborhood_attention_branch(nq, nk, nv, na_rpb),
    )
    merged = jax.nn.sigmoid(mg[..., :D] + gate_b[0]) * (outs[0].astype(u.dtype) @ w_branch[0])
    for i in range(1, N_BRANCH):
        gate = jax.nn.sigmoid(mg[..., i * D:(i + 1) * D] + gate_b[i])
        merged = merged + gate * (outs[i].astype(u.dtype) @ w_branch[i])
    return merged @ w_out


def expert_choice_ffn(u, w_router, w_gate_e, w_up_e, w_down_e):
    B, L, D = u.shape
    T = B * L
    cap = EC_CAPACITY * T // EC_EXPERTS
    xt = u.reshape(T, D)
    aff = jax.nn.softmax((xt @ w_router).astype(jnp.float32), axis=-1)
    g, idx = lax.top_k(aff.T, cap)
    xe = xt[idx]
    h = jax.nn.silu(jnp.einsum('ecd,edf->ecf', xe, w_gate_e)) * jnp.einsum('ecd,edf->ecf', xe, w_up_e)
    ye = jnp.einsum('ecf,efd->ecd', h, w_down_e) * g[..., None].astype(u.dtype)
    out = jnp.zeros_like(xt).at[idx.reshape(-1)].add(ye.reshape(-1, D))
    return out.reshape(B, L, D)


def encoder_trunk(x, norm_mix, w_in, gate_b, ret_decay, ret_gn_w, dn_conv_w, dn_A_log, dn_dt_bias,
                  dn_norm_w, na_rpb, w_branch, w_out, norm_ffn, w_router, w_gate_e, w_up_e, w_down_e,
                  final_norm):
    h = x
    for l in range(DEPTH):
        u = rms_norm(h, norm_mix[l])
        h = h + mixer_block(u, w_in[l], gate_b[l], ret_decay[l], ret_gn_w[l], dn_conv_w[l], dn_A_log[l],
                            dn_dt_bias[l], dn_norm_w[l], na_rpb[l], w_branch[l], w_out[l])
        u = rms_norm(h, norm_ffn[l])
        h = h + expert_choice_ffn(u, w_router[l], w_gate_e[l], w_up_e[l], w_down_e[l])
    return rms_norm(h, final_norm)


def setup_inputs(seed: int = 0) -> dict:
    key = jax.random.key(seed)
    ks = jax.random.split(key, 24)
    f32 = jnp.float32
    D = D_MODEL

    def nrm(k, shape, scale):
        return jax.random.normal(k, shape, f32) * scale

    qkv_w = 2 * DN_H * DN_DK + DN_H * DN_DV
    dt = jnp.exp(jax.random.uniform(ks[7], (DEPTH, 2, DN_H), f32, math.log(1e-3), math.log(1e-1)))
    ret_init = jnp.log(2.0 ** (5.0 + jnp.arange(RET_H, dtype=f32)) - 1.0)
    return {
        'x_prompt': nrm(ks[0], (BATCH, SEQ, D), 1.0),
        'x_sample': nrm(ks[1], (DEC_BATCH, DEC_SEQ, D), 1.0),
        'norm_mix': 1.0 + nrm(ks[2], (DEPTH, D), 0.05),
        'w_in': nrm(ks[3], (DEPTH, D, IN_WIDTH), D ** -0.5),
        'gate_b': nrm(ks[4], (DEPTH, N_BRANCH, D), 0.01),
        'ret_decay': ret_init[None, None, :] + nrm(ks[5], (DEPTH, 2, RET_H), 0.1),
        'ret_gn_w': 1.0 + nrm(ks[6], (DEPTH, RET_H * RET_DV), 0.05),
        'dn_conv_w': nrm(ks[8], (DEPTH, DN_CONV, qkv_w), DN_CONV ** -0.5),
        'dn_A_log': jnp.log(jax.random.uniform(ks[9], (DEPTH, 2, DN_H), f32, 1.0, 16.0)),
        'dn_dt_bias': dt + jnp.log(-jnp.expm1(-dt)),
        'dn_norm_w': 1.0 + nrm(ks[10], (DEPTH, DN_DV), 0.05),
        'na_rpb': nrm(ks[11], (DEPTH, NA_H, 2 * NA_WIN_R - 1, 2 * NA_WIN_C - 1), 0.02),
        'w_branch': nrm(ks[12], (DEPTH, N_BRANCH, BRANCH_W, D), BRANCH_W ** -0.5),
        'w_out': nrm(ks[13], (DEPTH, D, D), D ** -0.5),
        'norm_ffn': 1.0 + nrm(ks[14], (DEPTH, D), 0.05),
        'w_router': nrm(ks[15], (DEPTH, D, EC_EXPERTS), D ** -0.5),
        'w_gate_e': nrm(ks[16], (DEPTH, EC_EXPERTS, D, EC_FF), D ** -0.5),
        'w_up_e': nrm(ks[17], (DEPTH, EC_EXPERTS, D, EC_FF), D ** -0.5),
        'w_down_e': nrm(ks[18], (DEPTH, EC_EXPERTS, EC_FF, D), EC_FF ** -0.5),
        'final_norm': 1.0 + nrm(ks[19], (D,), 0.05),
    }


def reference(x_prompt, x_sample, norm_mix, w_in, gate_b, ret_decay, ret_gn_w, dn_conv_w, dn_A_log,
              dn_dt_bias, dn_norm_w, na_rpb, w_branch, w_out, norm_ffn, w_router, w_gate_e, w_up_e,
              w_down_e, final_norm):
    y_prompt = encoder_trunk(x_prompt, norm_mix, w_in, gate_b, ret_decay, ret_gn_w, dn_conv_w, dn_A_log,
                             dn_dt_bias, dn_norm_w, na_rpb, w_branch, w_out, norm_ffn, w_router, w_gate_e,
                             w_up_e, w_down_e, final_norm)
    y_sample = encoder_trunk(x_sample, norm_mix, w_in, gate_b, ret_decay, ret_gn_w, dn_conv_w, dn_A_log,
                             dn_dt_bias, dn_norm_w, na_rpb, w_branch, w_out, norm_ffn, w_router, w_gate_e,
                             w_up_e, w_down_e, final_norm)
    return (y_prompt, y_sample)
```

```python
import functools
import math

import jax
import jax.numpy as jnp
from jax import lax
from jax.experimental import pallas as pl
from jax.experimental.pallas import tpu as pltpu

F32 = jnp.float32
BF16 = jnp.bfloat16

D_MODEL = 1024
DEPTH = 4
GRID_W = 64
RET_H, RET_DK, RET_DV, RET_CHUNK = 4, 128, 128, 128
ROPE_BASE = 10000.0
DN_H, DN_DK, DN_DV, DN_CONV, DN_CHUNK = 4, 128, 128, 3, 64
NA_H, NA_DH, NA_WIN_R, NA_WIN_C = 8, 64, 8, 16
N_BRANCH = 3
BRANCH_W = 512
EC_EXPERTS, EC_CAPACITY = 16, 2
NORM_EPS = 1e-6
GN_EPS = 1e-5
NEG_BIG = -1e30

CB = 512
COL_MG, COL_RQ, COL_RK, COL_RV, COL_RG = 0, 6, 7, 8, 9
COL_DQ, COL_DK, COL_DV, COL_DG = 10, 11, 12, 13
COL_NQ, COL_NK, COL_NV = 14, 15, 16
P_WIDTH = 17 * CB
_SRC = dict(rq=0, rk=512, rv=1024, rg=1536, dqkv=2048, dg=3584, db=4096, da=4104,
            nq=4112, nk=4624, nv=5136, mg=5648)

V7X_VMEM_LIMIT = 56 * 1024 * 1024


def _cparams(sem):
    return pltpu.CompilerParams(dimension_semantics=sem, vmem_limit_bytes=V7X_VMEM_LIMIT)


def _mm(a, b):
    return jnp.dot(a.astype(BF16), b.astype(BF16), preferred_element_type=F32)


def _mm_nt(a, b):
    return lax.dot_general(a.astype(BF16), b.astype(BF16), (((1,), (1,)), ((), ())),
                           preferred_element_type=F32)


def _split3(x):
    hi = x.astype(BF16)
    r = x - hi.astype(F32)
    mid = r.astype(BF16)
    lo = (r - mid.astype(F32)).astype(BF16)
    return hi, mid, lo


def _mm_exact_rhs(a_bf16, x):
    hi, mid, lo = _split3(x)
    d = functools.partial(jnp.dot, preferred_element_type=F32)
    return d(a_bf16, hi) + d(a_bf16, mid) + d(a_bf16, lo)


def _mm3(a, b):
    ah = a.astype(BF16)
    al = (a - ah.astype(F32)).astype(BF16)
    bh = b.astype(BF16)
    bl = (b - bh.astype(F32)).astype(BF16)
    d = functools.partial(jnp.dot, preferred_element_type=F32)
    return d(ah, bh) + (d(ah, bl) + d(al, bh))


def _sigmoid(x):
    return 1.0 / (1.0 + jnp.exp(-x))


def _silu(x):
    return x * _sigmoid(x)


def _rms(x, w_row):
    return x * lax.rsqrt(jnp.mean(x * x, axis=-1, keepdims=True) + NORM_EPS) * w_row


def _norm_kernel(x_ref, w_ref, o_ref):
    o_ref[...] = _rms(x_ref[...], w_ref[...]).astype(o_ref.dtype)


def rmsnorm(x2d, w, out_dtype, tm=512):
    T = x2d.shape[0]
    return pl.pallas_call(
        _norm_kernel,
        grid=(T // tm,),
        in_specs=[pl.BlockSpec((tm, D_MODEL), lambda i: (i, 0)),
                  pl.BlockSpec((1, D_MODEL), lambda i: (0, 0))],
        out_specs=pl.BlockSpec((tm, D_MODEL), lambda i: (i, 0)),
        out_shape=jax.ShapeDtypeStruct((T, D_MODEL), out_dtype),
        compiler_params=_cparams(("parallel",)),
        name="rmsnorm",
    )(x2d, w.reshape(1, D_MODEL))


IN_TN = P_WIDTH // 2
IN_CH = 256


def _inproj_kernel(u_ref, w_ref, o_ref):
    u = u_ref[...]
    for c in range(0, IN_TN, IN_CH):
        o_ref[:, c:c + IN_CH] = jnp.dot(u, w_ref[:, c:c + IN_CH], preferred_element_type=F32)


def inproj(u, w_main, tm=512):
    T = u.shape[0]
    return pl.pallas_call(
        _inproj_kernel,
        grid=(2, T // tm),
        in_specs=[pl.BlockSpec((tm, D_MODEL), lambda j, i: (i, 0)),
                  pl.BlockSpec((D_MODEL, IN_TN), lambda j, i: (0, j))],
        out_specs=pl.BlockSpec((tm, IN_TN), lambda j, i: (i, j)),
        out_shape=jax.ShapeDtypeStruct((T, P_WIDTH), F32),
        compiler_params=_cparams(("arbitrary", "arbitrary")),
        name="inproj",
    )(u, w_main)


def pack_w_in(w_in_l):
    s = _SRC
    segs = [w_in_l[:, s["mg"]:s["mg"] + 3 * D_MODEL],
            w_in_l[:, s["rq"]:s["rq"] + 2048],
            w_in_l[:, s["dqkv"]:s["dqkv"] + 2048],
            w_in_l[:, s["nq"]:s["nq"] + 1536]]
    w_main = jnp.concatenate(segs, axis=1).astype(BF16)
    w_ba = jnp.pad(w_in_l[:, s["db"]:s["db"] + 16], ((0, 0), (0, 112))).astype(BF16)
    return w_main, w_ba


def _rope_tables(L):
    half = RET_DK // 2
    inv = ROPE_BASE ** (-jnp.arange(half, dtype=F32) / half)
    ang = jnp.arange(L, dtype=F32)[:, None] * inv[None, :]
    cos, sin = jnp.cos(ang), jnp.sin(ang)
    return jnp.concatenate([cos, cos], axis=1), jnp.concatenate([-sin, sin], axis=1)


def _rope(x, cosf, sinf):
    return x * cosf + pltpu.roll(x, RET_DK // 2, 1) * sinf


def _row_iota(shape):
    return lax.broadcasted_iota(jnp.int32, shape, 0).astype(F32)


def _ret_bwd_kernel(lg_ref, q_ref, k_ref, v_ref, cos_ref, sin_ref, o_ref, s_ref):
    C = RET_CHUNK

    @pl.when(pl.program_id(1) == 0)
    def _():
        s_ref[...] = jnp.zeros_like(s_ref)

    cosf, sinf = cos_ref[...], sin_ref[...]
    ri = _row_iota((C, RET_DK))
    for h in range(RET_H):
        sl = slice(h * RET_DK, (h + 1) * RET_DK)
        lg = lg_ref[1, h]
        q = _rope(q_ref[:, sl], cosf, sinf)
        k = _rope(k_ref[:, sl], cosf, sinf) * (RET_DK ** -0.5)
        v = v_ref[:, sl]
        qd = q * jnp.exp(lg * (C - ri))
        kd = k * jnp.exp(lg * ri)
        S = s_ref[h]
        o_ref[:, sl] = _mm(qd, S)
        s_ref[h] = S * jnp.exp(lg * C) + _mm(kd.T, v)


def _ret_fwd_kernel(lg_ref, q_ref, k_ref, v_ref, g_ref, ib_ref, cos_ref, sin_ref, gnw_ref, o_ref,
                    s_ref, dm_ref):
    C = RET_CHUNK

    @pl.when(pl.program_id(1) == 0)
    def _():
        s_ref[...] = jnp.zeros_like(s_ref)
        diff = (lax.broadcasted_iota(jnp.int32, (C, C), 0)
                - lax.broadcasted_iota(jnp.int32, (C, C), 1)).astype(F32)
        for h in range(RET_H):
            dm_ref[h] = jnp.where(diff >= 0, jnp.exp(lg_ref[0, h] * jnp.maximum(diff, 0.0)),
                                  jnp.exp(lg_ref[1, h] * jnp.maximum(-diff, 0.0)))

    cosf, sinf = cos_ref[...], sin_ref[...]
    ri = _row_iota((C, RET_DK))
    for h in range(RET_H):
        sl = slice(h * RET_DK, (h + 1) * RET_DK)
        lg = lg_ref[0, h]
        q = _rope(q_ref[:, sl], cosf, sinf)
        k = _rope(k_ref[:, sl], cosf, sinf) * (RET_DK ** -0.5)
        v = v_ref[:, sl]
        s = _mm_nt(q, k) * dm_ref[h]
        S = s_ref[h]
        r = _mm(s, v) + _mm(q * jnp.exp(lg * (ri + 1.0)), S) + ib_ref[:, sl]
        s_ref[h] = S * jnp.exp(lg * C) + _mm((k * jnp.exp(lg * (C - 1.0 - ri))).T, v)
        mu = jnp.mean(r, axis=-1, keepdims=True)
        rc = r - mu
        var = jnp.mean(rc * rc, axis=-1, keepdims=True)
        rn = rc * lax.rsqrt(var + GN_EPS) * gnw_ref[:, sl]
        o_ref[:, sl] = (_silu(g_ref[:, sl]) * rn).astype(o_ref.dtype)


def retention(p, lg, gn_w, cosf, sinf, B, L):
    C = RET_CHUNK
    N = L // C
    T = B * L
    smem = pl.BlockSpec(memory_space=pltpu.SMEM)

    def pspec(col, rev):
        if rev:
            return pl.BlockSpec((C, CB), lambda b, n: (b * N + N - 1 - n, col))
        return pl.BlockSpec((C, CB), lambda b, n: (b * N + n, col))

    def tspec(rev):
        if rev:
            return pl.BlockSpec((C, RET_DK), lambda b, n: (N - 1 - n, 0))
        return pl.BlockSpec((C, RET_DK), lambda b, n: (n, 0))

    inter_b = pl.pallas_call(
        _ret_bwd_kernel,
        grid=(B, N),
        in_specs=[smem, pspec(COL_RQ, True), pspec(COL_RK, True), pspec(COL_RV, True),
                  tspec(True), tspec(True)],
        out_specs=pl.BlockSpec((C, CB), lambda b, n: (b * N + N - 1 - n, 0)),
        out_shape=jax.ShapeDtypeStruct((T, CB), F32),
        scratch_shapes=[pltpu.VMEM((RET_H, RET_DK, RET_DV), F32)],
        compiler_params=_cparams(("arbitrary", "arbitrary")),
        name="ret_bwd",
    )(lg, p, p, p, cosf, sinf)

    return pl.pallas_call(
        _ret_fwd_kernel,
        grid=(B, N),
        in_specs=[smem, pspec(COL_RQ, False), pspec(COL_RK, False), pspec(COL_RV, False),
                  pspec(COL_RG, False), pl.BlockSpec((C, CB), lambda b, n: (b * N + n, 0)),
                  tspec(False), tspec(False), pl.BlockSpec((1, CB), lambda b, n: (0, 0))],
        out_specs=pl.BlockSpec((C, CB), lambda b, n: (b * N + n, 0)),
        out_shape=jax.ShapeDtypeStruct((T, CB), BF16),
        scratch_shapes=[pltpu.VMEM((RET_H, RET_DK, RET_DV), F32),
                        pltpu.VMEM((RET_H, C, C), F32)],
        compiler_params=_cparams(("arbitrary", "arbitrary")),
        name="ret_fwd",
    )(lg, p, p, p, p, inter_b, cosf, sinf, gn_w.reshape(1, CB))


DN_PREP_TM = 256


def _softplus(x):
    return jnp.maximum(x, 0.0) + jnp.log1p(jnp.exp(-jnp.abs(x)))


def _dn_prep_kernel(n_seq_tiles, u_ref, wba_ref, prm_ref, cw_ref,
                    q_ref, qp_ref, qn_ref, k_ref, kp_ref, kn_ref, v_ref, vp_ref, vn_ref,
                    qo_ref, ko_ref, vo_ref, bg_ref):
    tm = DN_PREP_TM
    i = pl.program_id(0)
    first = (i % n_seq_tiles) == 0
    last = (i % n_seq_tiles) == n_seq_tiles - 1
    rows = lax.broadcasted_iota(jnp.int32, (tm, CB), 0)

    def conv_silu(x_ref, xp_ref, xn_ref, seg):
        x = x_ref[...]
        prev_row = jnp.where(first, 0.0, xp_ref[7:8, :])
        next_row = jnp.where(last, 0.0, xn_ref[0:1, :])
        x_prev = jnp.where(rows == 0, prev_row, pltpu.roll(x, 1, 0))
        x_next = jnp.where(rows == tm - 1, next_row, pltpu.roll(x, tm - 1, 0))
        cs = slice(seg * CB, (seg + 1) * CB)
        y = x_prev * cw_ref[0:1, cs] + x * cw_ref[1:2, cs] + x_next * cw_ref[2:3, cs]
        return _silu(y)

    def l2n(y, scale):
        outs = []
        for h in range(DN_H):
            s = y[:, h * DN_DK:(h + 1) * DN_DK]
            outs.append(s * (lax.rsqrt(jnp.sum(s * s, axis=-1, keepdims=True) + 1e-6) * scale))
        return jnp.concatenate(outs, axis=1)

    qo_ref[...] = l2n(conv_silu(q_ref, qp_ref, qn_ref, 0), DN_DK ** -0.5)
    ko_ref[...] = l2n(conv_silu(k_ref, kp_ref, kn_ref, 1), 1.0)
    vo_ref[...] = conv_silu(v_ref, vp_ref, vn_ref, 2)

    ba = jnp.dot(u_ref[...], wba_ref[...], preferred_element_type=F32)
    lane = lax.broadcasted_iota(jnp.int32, ba.shape, 1)
    beta = _sigmoid(ba)
    g = -prm_ref[0:1, :] * _softplus(ba + prm_ref[1:2, :])
    bg_ref[...] = jnp.where(lane < 2 * DN_H, beta, g)


def dn_prep(p, u, w_ba, conv_w, A_log, dt_bias, L):
    T = p.shape[0]
    tm = DN_PREP_TM
    nb8 = T // 8
    prm = jnp.zeros((2, 128), F32)
    prm = prm.at[0, 8:16].set(jnp.exp(A_log.astype(F32)).reshape(-1))
    prm = prm.at[1, 8:16].set(dt_bias.astype(F32).reshape(-1))

    def trio(col):
        return [pl.BlockSpec((tm, CB), lambda i: (i, col)),
                pl.BlockSpec((8, CB), lambda i: (jnp.maximum(i * (tm // 8) - 1, 0), col)),
                pl.BlockSpec((8, CB), lambda i: (jnp.minimum((i + 1) * (tm // 8), nb8 - 1), col))]

    full = lambda shape: pl.BlockSpec(shape, lambda i: (0,) * len(shape))
    out_sd = jax.ShapeDtypeStruct((T, CB), F32)
    return pl.pallas_call(
        functools.partial(_dn_prep_kernel, L // tm),
        grid=(T // tm,),
        in_specs=[pl.BlockSpec((tm, D_MODEL), lambda i: (i, 0)), full((D_MODEL, 128)), full((2, 128)),
                  full((DN_CONV, 3 * CB))] + trio(COL_DQ) + trio(COL_DK) + trio(COL_DV),
        out_specs=[pl.BlockSpec((tm, CB), lambda i: (i, 0))] * 3 + [pl.BlockSpec((tm, 128), lambda i: (i, 0))],
        out_shape=[out_sd, out_sd, out_sd, jax.ShapeDtypeStruct((T, 128), F32)],
        compiler_params=_cparams(("parallel",)),
        name="dn_prep",
    )(u, w_ba, prm, conv_w.astype(F32), p, p, p, p, p, p, p, p, p)


_M_INCL_F, _M_STRICT_F, _M_INCL_B, _M_STRICT_B, _M_SAME8, _M_LEV8, _M_LEV16, _M_LEV32, _M_EYE, _M_PAIR = range(10)


def _dn_masks():
    import numpy as np
    n = DN_H * DN_CHUNK
    r = np.arange(n)[:, None]
    c = np.arange(n)[None, :]
    same_head = (r // DN_CHUNK) == (c // DN_CHUNK)
    ir, ic = r % DN_CHUNK, c % DN_CHUNK
    lev = lambda s: ((r // (2 * s)) == (c // (2 * s))) & ((r // s) != (c // s))
    ms = [same_head & (ir >= ic), same_head & (ir > ic), same_head & (ir <= ic), same_head & (ir < ic),
          (r // 8) == (c // 8), lev(8), lev(16), lev(32), r == c, (r // 128) == (c // 128)]
    m32 = np.stack(ms).astype(np.float32)
    m16 = np.stack([ms[_M_INCL_F], ms[_M_INCL_B], same_head]).astype(np.float32)
    return jnp.asarray(m32), jnp.asarray(m16, dtype=BF16)


def _hstack(x):
    return jnp.concatenate([x[:, h * DN_DK:(h + 1) * DN_DK] for h in range(DN_H)], axis=0)


def _hunstack(xs):
    return jnp.concatenate([xs[h * DN_CHUNK:(h + 1) * DN_CHUNK] for h in range(DN_H)], axis=1)


def _dn_scan_kernel(m_ref, mb_ref,
                    qf_ref, kf_ref, vf_ref, bgf_ref, qb_ref, kb_ref, vb_ref, bgb_ref,
                    of_ref, ob_ref, s_ref):
    C = DN_CHUNK

    @pl.when(pl.program_id(1) == 0)
    def _():
        s_ref[...] = jnp.zeros_like(s_ref)

    eye = m_ref[_M_EYE]
    pair = m_ref[_M_PAIR]
    ones_bd = mb_ref[2]
    for d, (q_ref, k_ref, v_ref, bg_ref, o_ref) in enumerate(
            ((qf_ref, kf_ref, vf_ref, bgf_ref, of_ref), (qb_ref, kb_ref, vb_ref, bgb_ref, ob_ref))):
        incl = m_ref[_M_INCL_F if d == 0 else _M_INCL_B]
        strict = m_ref[_M_STRICT_F if d == 0 else _M_STRICT_B]
        Qs, Ks, Vs = _hstack(q_ref[...]), _hstack(k_ref[...]), _hstack(v_ref[...])
        bg = bg_ref[...]
        col = lambda j: jnp.concatenate([bg[:, j + h:j + h + 1] for h in range(DN_H)], axis=0)
        beta = col(d * DN_H)
        gB = jnp.broadcast_to(col(2 * DN_H + d * DN_H), (DN_H * C, 128))
        gc = _mm_exact_rhs(mb_ref[d], gB)
        gt = _mm_exact_rhs(ones_bd, gB)
        gc_row = gc.T[0:1, :]
        diff = jnp.concatenate([gc, gc], axis=1) - gc_row
        decay = jnp.exp(jnp.minimum(diff, 0.0)) * incl
        kb = Ks * beta
        G = _mm_nt(jnp.concatenate([kb, Qs], axis=0), Ks)
        A = G[:DN_H * C] * (decay * strict)
        attn = G[DN_H * C:] * decay
        P = -(A * m_ref[_M_SAME8])
        X = eye + P
        P = _mm3(P, P)
        X = X + _mm3(X, P)
        P = _mm3(P, P)
        X = X + _mm3(X, P)
        for lev in (_M_LEV8, _M_LEV16, _M_LEV32):
            X = X - _mm3(X, _mm3(A * m_ref[lev], X))
        egc = jnp.exp(gc)
        sol = _mm3(X, jnp.concatenate([Vs * beta, kb * egc], axis=1))
        u_s, w_s = sol[:, :DN_DV], sol[:, DN_DV:]
        w_l = _hunstack(w_s)
        qd_l = _hunstack(Qs * egc)
        kd_l = _hunstack(Ks * jnp.exp(gt - gc))
        wS, qS = [], []
        for pr in range(2):
            sl = slice(pr * 256, (pr + 1) * 256)
            res = _mm(jnp.concatenate([w_l[:, sl], qd_l[:, sl]], axis=0), s_ref[d, pr])
            wS.append(res[:C])
            qS.append(res[C:])
        vnew_s = u_s - _hstack(jnp.concatenate(wS, axis=1))
        o_s = _hstack(jnp.concatenate(qS, axis=1)) + _mm(attn, vnew_s)
        o_ref[...] = _hunstack(o_s)
        vnew_l = _hunstack(vnew_s)
        egt = jnp.exp(gt)
        for pr in range(2):
            sl = slice(pr * 256, (pr + 1) * 256)
            upd = _mm(kd_l[:, sl].T, vnew_l[:, sl])
            ha, hb = egt[(2 * pr) * C:(2 * pr + 1) * C], egt[(2 * pr + 1) * C:(2 * pr + 2) * C]
            cd = jnp.concatenate([ha, ha, hb, hb], axis=0)
            s_ref[d, pr] = s_ref[d, pr] * jnp.concatenate([cd, cd], axis=1) + upd * pair


def dn_scan(qn, kn, vs, bg, B, L):
    C = DN_CHUNK
    N = L // C
    T = B * L
    m32, m16 = _dn_masks()
    fwd = lambda w: pl.BlockSpec((C, w), lambda b, n: (b * N + n, 0))
    bwd = lambda w: pl.BlockSpec((C, w), lambda b, n: (b * N + N - 1 - n, 0))
    full = lambda a: pl.BlockSpec(a.shape, lambda b, n: (0,) * a.ndim)
    out_sd = jax.ShapeDtypeStruct((T, CB), F32)
    return pl.pallas_call(
        _dn_scan_kernel,
        grid=(B, N),
        in_specs=[full(m32), full(m16), fwd(CB), fwd(CB), fwd(CB), fwd(128), bwd(CB), bwd(CB), bwd(CB), bwd(128)],
        out_specs=[fwd(CB), bwd(CB)],
        out_shape=[out_sd, out_sd],
        scratch_shapes=[pltpu.VMEM((2, 2, 256, 256), F32)],
        compiler_params=_cparams(("arbitrary", "arbitrary")),
        name="dn_scan",
    )(m32, m16, qn, kn, vs, bg, qn, kn, vs, bg)


def _na_bias_table(rpb):
    qc = jnp.arange(GRID_W)[:, None]
    kc = jnp.arange(GRID_W)[None, :]
    cs = jnp.clip(qc - NA_WIN_C // 2, 0, GRID_W - NA_WIN_C)
    ok = (kc >= cs) & (kc < cs + NA_WIN_C)
    dc = jnp.clip(kc - qc, -(NA_WIN_C - 1), NA_WIN_C - 1) + NA_WIN_C - 1
    delta = jnp.arange(NA_WIN_R)[:, None]
    j = jnp.arange(NA_WIN_R)[None, :]
    dr = j - delta + NA_WIN_R - 1
    t = rpb.astype(F32)[:, dr][:, :, :, dc]
    t = jnp.where(ok[None, None, None], t, NEG_BIG)
    t = t.transpose(1, 0, 3, 2, 4)
    return t.reshape(NA_WIN_R, NA_H, GRID_W, NA_WIN_R * GRID_W)


def _na_kernel(q_ref, k_ref, v_ref, bias_ref, o_ref):
    nk = NA_WIN_R * GRID_W
    q = q_ref[...] * (NA_DH ** -0.5)
    kw = k_ref[...].reshape(nk, CB)
    vw = v_ref[...].reshape(nk, CB)
    outs = []
    for h in range(NA_H):
        sl = slice(h * NA_DH, (h + 1) * NA_DH)
        s = _mm_nt(q[:, sl], kw[:, sl]) + bias_ref[h]
        e = jnp.exp(s - jnp.max(s, axis=-1, keepdims=True))
        outs.append(_mm(e, vw[:, sl]) / jnp.sum(e, axis=-1, keepdims=True))
    o_ref[...] = jnp.concatenate(outs, axis=1).astype(o_ref.dtype)


def neighborhood_attention(p, bias, B, L):
    rows = L // GRID_W
    assert rows >= NA_WIN_R
    p3 = p.reshape(B * rows, GRID_W, P_WIDTH)

    def rs(r):
        return jnp.clip(r - NA_WIN_R // 2, 0, rows - NA_WIN_R)

    win = lambda col: pl.BlockSpec((pl.Element(NA_WIN_R), pl.Element(GRID_W), pl.Element(CB)),
                                   lambda b, r: (b * rows + rs(r), 0, col * CB))
    out = pl.pallas_call(
        _na_kernel,
        grid=(B, rows),
        in_specs=[pl.BlockSpec((None, GRID_W, CB), lambda b, r: (b * rows + r, 0, COL_NQ)),
                  win(COL_NK), win(COL_NV),
                  pl.BlockSpec((None, NA_H, GRID_W, NA_WIN_R * GRID_W), lambda b, r: (r - rs(r), 0, 0, 0))],
        out_specs=pl.BlockSpec((None, GRID_W, CB), lambda b, r: (b * rows + r, 0, 0)),
        out_shape=jax.ShapeDtypeStruct((B * rows, GRID_W, CB), BF16),
        compiler_params=_cparams(("parallel", "arbitrary")),
        name="natten",
    )(p3, p3, p3, bias)
    return out.reshape(B * L, CB)


MERGE_TM = 256


def _merge_kernel(h_ref, ret_ref, of_ref, ob_ref, dg_ref, na_ref, mg0_ref, mg1_ref, mg2_ref,
                  gb_ref, wb_ref, wo_ref, dnw_ref, nf_ref, wr_ref,
                  ho_ref, uo_ref, aff_ref):
    o = of_ref[...] + ob_ref[...]
    dn = []
    for h in range(DN_H):
        sl = slice(h * DN_DV, (h + 1) * DN_DV)
        oh = o[:, sl]
        dn.append(oh * lax.rsqrt(jnp.mean(oh * oh, axis=-1, keepdims=True) + NORM_EPS))
    dn = jnp.concatenate(dn, axis=1) * dnw_ref[...] * _silu(dg_ref[...])
    branches = (ret_ref[...], dn.astype(BF16), na_ref[...])
    merged = None
    for i, (br, mg_ref) in enumerate(zip(branches, (mg0_ref, mg1_ref, mg2_ref))):
        t = _sigmoid(mg_ref[...] + gb_ref[i:i + 1, :]) * jnp.dot(br, wb_ref[i], preferred_element_type=F32)
        merged = t if merged is None else merged + t
    h_new = h_ref[...] + jnp.dot(merged.astype(BF16), wo_ref[...], preferred_element_type=F32)
    ho_ref[...] = h_new
    u2 = _rms(h_new, nf_ref[...])
    uo_ref[...] = u2
    logits = _mm3(u2, wr_ref[...])
    lane = lax.broadcasted_iota(jnp.int32, logits.shape, 1)
    logits = jnp.where(lane < EC_EXPERTS, logits, NEG_BIG)
    e = jnp.exp(logits - jnp.max(logits, axis=-1, keepdims=True))
    aff = e / jnp.sum(e, axis=-1, keepdims=True)
    aff_ref[...] = aff.T[0:EC_EXPERTS, :]


def merge(h, ret, of, ob, na, p, gate_b_l, w_branch_l, w_out_l, dn_norm_w_l, norm_ffn_l, w_router_l):
    T = h.shape[0]
    tm = MERGE_TM
    row = lambda w, col=0: pl.BlockSpec((tm, w), lambda i: (i, col))
    full = lambda shape: pl.BlockSpec(shape, lambda i: (0,) * len(shape))
    wr = jnp.pad(w_router_l.astype(F32), ((0, 0), (0, 128 - EC_EXPERTS)))
    return pl.pallas_call(
        _merge_kernel,
        grid=(T // tm,),
        in_specs=[row(D_MODEL), row(CB), row(CB), row(CB), row(CB, COL_DG), row(CB),
                  row(D_MODEL, 0), row(D_MODEL, 1), row(D_MODEL, 2),
                  full((N_BRANCH, D_MODEL)), full((N_BRANCH, BRANCH_W, D_MODEL)), full((D_MODEL, D_MODEL)),
                  full((1, CB)), full((1, D_MODEL)), full((D_MODEL, 128))],
        out_specs=[row(D_MODEL), row(D_MODEL), pl.BlockSpec((EC_EXPERTS, tm), lambda i: (0, i))],
        out_shape=[jax.ShapeDtypeStruct((T, D_MODEL), F32), jax.ShapeDtypeStruct((T, D_MODEL), F32),
                   jax.ShapeDtypeStruct((EC_EXPERTS, T), F32)],
        compiler_params=_cparams(("parallel",)),
        name="merge",
    )(h, ret, of, ob, p, na, p, p, p,
      gate_b_l.astype(F32), w_branch_l.astype(BF16), w_out_l.astype(BF16),
      jnp.tile(dn_norm_w_l.astype(F32), DN_H).reshape(1, CB), norm_ffn_l.astype(F32).reshape(1, D_MODEL), wr)


FFN_TS = 256


def _ffn_kernel(idx_ref, idxn_ref, x_hbm, g_ref, wg_ref, wu_ref, wd_ref, y_ref, xbuf, sem):
    ts = FFN_TS
    k = pl.program_id(0)
    nk = pl.num_programs(0)
    slot = k % 2

    def row_copy(src_row, s, r):
        return pltpu.make_async_copy(x_hbm.at[pl.ds(src_row, 1), :], xbuf.at[s, pl.ds(r, 1), :], sem.at[s])

    def issue(iref, s):
        def body(r, c):
            row_copy(iref[0, 0, r], s, r).start()
            return c
        lax.fori_loop(0, ts, body, 0, unroll=8)

    @pl.when(k == 0)
    def _():
        issue(idx_ref, 0)

    @pl.when(k + 1 < nk)
    def _():
        issue(idxn_ref, 1 - slot)

    def wbody(r, c):
        row_copy(0, slot, r).wait()
        return c
    lax.fori_loop(0, ts, wbody, 0, unroll=8)

    x = xbuf[slot].astype(BF16)
    hg = jnp.dot(x, wg_ref[0], preferred_element_type=F32)
    hu = jnp.dot(x, wu_ref[0], preferred_element_type=F32)
    hh = (_silu(hg) * hu).astype(BF16)
    y_ref[...] = jnp.dot(hh, wd_ref[0], preferred_element_type=F32) * g_ref[...]


def moe_ffn(idx, g, x, wg, wu, wd):
    E, cap = idx.shape
    ts = FFN_TS
    nt = cap // ts
    K = E * nt
    idx3 = idx.reshape(K, 1, ts).astype(jnp.int32)
    wspec = pl.BlockSpec((1, D_MODEL, D_MODEL), lambda k: (k // nt, 0, 0))
    return pl.pallas_call(
        _ffn_kernel,
        grid=(K,),
        in_specs=[pl.BlockSpec((1, 1, ts), lambda k: (k, 0, 0), memory_space=pltpu.SMEM),
                  pl.BlockSpec((1, 1, ts), lambda k: (jnp.minimum(k + 1, K - 1), 0, 0), memory_space=pltpu.SMEM),
                  pl.BlockSpec(memory_space=pl.ANY),
                  pl.BlockSpec((ts, 1), lambda k: (k, 0)),
                  wspec, wspec, wspec],
        out_specs=pl.BlockSpec((ts, D_MODEL), lambda k: (k, 0)),
        out_shape=jax.ShapeDtypeStruct((E * cap, D_MODEL), F32),
        scratch_shapes=[pltpu.VMEM((2, ts, D_MODEL), F32), pltpu.SemaphoreType.DMA((2,))],
        compiler_params=_cparams(("arbitrary",)),
        name="moe_ffn",
    )(idx3, idx3, x, g.reshape(E * cap, 1).astype(F32), wg, wu, wd)


def _add_norm_kernel(emit_h, h_ref, m_ref, w_ref, *out_refs):
    h_new = h_ref[...] + m_ref[...]
    if emit_h:
        out_refs[0][...] = h_new
    out_refs[-1][...] = _rms(h_new, w_ref[...]).astype(out_refs[-1].dtype)


def add_norm(h, m, w, emit_h, norm_dtype, tm=512):
    T = h.shape[0]
    row = pl.BlockSpec((tm, D_MODEL), lambda i: (i, 0))
    outs = ([jax.ShapeDtypeStruct((T, D_MODEL), F32)] if emit_h else []) + [jax.ShapeDtypeStruct((T, D_MODEL), norm_dtype)]
    return pl.pallas_call(
        functools.partial(_add_norm_kernel, emit_h),
        grid=(T // tm,),
        in_specs=[row, row, pl.BlockSpec((1, D_MODEL), lambda i: (0, 0))],
        out_specs=[row] * len(outs),
        out_shape=outs,
        compiler_params=_cparams(("parallel",)),
        name="add_norm",
    )(h, m, w.astype(F32).reshape(1, D_MODEL))


def _trunk(x, prm):
    B, L, _ = x.shape
    T = B * L
    cap = EC_CAPACITY * T // EC_EXPERTS
    h = x.reshape(T, D_MODEL)
    cosf, sinf = _rope_tables(L)
    u = rmsnorm(h, prm["norm_mix"][0].astype(F32), BF16)
    for l in range(DEPTH):
        p = inproj(u, prm["w_main"][l])
        ret = retention(p, prm["ret_lg"][l], prm["ret_gn_w"][l].astype(F32), cosf, sinf, B, L)
        qn, kn, vs, bg = dn_prep(p, u, prm["w_ba"][l], prm["dn_conv_w"][l], prm["dn_A_log"][l],
                                 prm["dn_dt_bias"][l], L)
        of, ob = dn_scan(qn, kn, vs, bg, B, L)
        na = neighborhood_attention(p, prm["na_bias"][l], B, L)
        h, u_ffn, aff_t = merge(h, ret, of, ob, na, p, prm["gate_b"][l], prm["w_branch"][l], prm["w_out"][l],
                                prm["dn_norm_w"][l], prm["norm_ffn"][l], prm["w_router"][l])
        g, idx = lax.top_k(aff_t, cap)
        y = moe_ffn(idx, g, u_ffn, prm["w_gate_e"][l], prm["w_up_e"][l], prm["w_down_e"][l])
        moe = jnp.zeros((T, D_MODEL), F32).at[idx.reshape(-1)].add(y)
        if l + 1 < DEPTH:
            h, u = add_norm(h, moe, prm["norm_mix"][l + 1], True, BF16)
        else:
            (out,) = add_norm(h, moe, prm["final_norm"], False, F32)
    return out.reshape(B, L, D_MODEL)


def kernel(x_prompt, x_sample, norm_mix, w_in, gate_b, ret_decay, ret_gn_w, dn_conv_w, dn_A_log, dn_dt_bias, dn_norm_w, na_rpb, w_branch, w_out, norm_ffn, w_router, w_gate_e, w_up_e, w_down_e, final_norm):
    packed = [pack_w_in(w_in[l]) for l in range(DEPTH)]
    prm = dict(
        norm_mix=norm_mix, gate_b=gate_b, ret_gn_w=ret_gn_w, dn_conv_w=dn_conv_w, dn_A_log=dn_A_log,
        dn_dt_bias=dn_dt_bias, dn_norm_w=dn_norm_w, w_branch=w_branch, w_out=w_out, norm_ffn=norm_ffn,
        w_router=w_router, final_norm=final_norm,
        w_main=[pk[0] for pk in packed], w_ba=[pk[1] for pk in packed],
        ret_lg=jax.nn.log_sigmoid(ret_decay.astype(F32)),
        na_bias=[_na_bias_table(na_rpb[l]) for l in range(DEPTH)],
        w_gate_e=w_gate_e.astype(BF16), w_up_e=w_up_e.astype(BF16), w_down_e=w_down_e.astype(BF16),
    )
    return (_trunk(x_prompt, prm), _trunk(x_sample, prm))
```

```python
import functools
import math

import jax
import jax.numpy as jnp
from jax import lax
from jax.experimental import pallas as pl
from jax.experimental.pallas import tpu as pltpu

F32 = jnp.float32
BF16 = jnp.bfloat16

D_MODEL = 1024
DEPTH = 4
GRID_W = 64
RET_H, RET_DK, RET_DV, RET_CHUNK = 4, 128, 128, 128
ROPE_BASE = 10000.0
DN_H, DN_DK, DN_DV, DN_CONV, DN_CHUNK = 4, 128, 128, 3, 64
NA_H, NA_DH, NA_WIN_R, NA_WIN_C = 8, 64, 8, 16
N_BRANCH = 3
BRANCH_W = 512
EC_EXPERTS, EC_CAPACITY = 16, 2
NORM_EPS = 1e-6
GN_EPS = 1e-5
NEG_BIG = -1e30

CB = 512
COL_MG, COL_RQ, COL_RK, COL_RV, COL_RG = 0, 6, 7, 8, 9
COL_DQ, COL_DK, COL_DV, COL_DG = 10, 11, 12, 13
COL_NQ, COL_NK, COL_NV = 14, 15, 16
P_WIDTH = 17 * CB
_SRC = dict(rq=0, rk=512, rv=1024, rg=1536, dqkv=2048, dg=3584, db=4096, da=4104,
            nq=4112, nk=4624, nv=5136, mg=5648)

V7X_VMEM_LIMIT = 56 * 1024 * 1024


def _cparams(sem):
    return pltpu.CompilerParams(dimension_semantics=sem, vmem_limit_bytes=V7X_VMEM_LIMIT)


def _mm(a, b):
    return jnp.dot(a.astype(BF16), b.astype(BF16), preferred_element_type=F32)


def _mm_nt(a, b):
    return lax.dot_general(a.astype(BF16), b.astype(BF16), (((1,), (1,)), ((), ())),
                           preferred_element_type=F32)


def _split3(x):
    hi = x.astype(BF16)
    r = x - hi.astype(F32)
    mid = r.astype(BF16)
    lo = (r - mid.astype(F32)).astype(BF16)
    return hi, mid, lo


def _mm_exact_rhs(a_bf16, x):
    hi, mid, lo = _split3(x)
    d = functools.partial(jnp.dot, preferred_element_type=F32)
    return d(a_bf16, hi) + d(a_bf16, mid) + d(a_bf16, lo)


def _mm3(a, b):
    ah = a.astype(BF16)
    al = (a - ah.astype(F32)).astype(BF16)
    bh = b.astype(BF16)
    bl = (b - bh.astype(F32)).astype(BF16)
    d = functools.partial(jnp.dot, preferred_element_type=F32)
    return d(ah, bh) + (d(ah, bl) + d(al, bh))


def _sigmoid(x):
    return 1.0 / (1.0 + jnp.exp(-x))


def _silu(x):
    return x * _sigmoid(x)


def _rms(x, w_row):
    return x * lax.rsqrt(jnp.mean(x * x, axis=-1, keepdims=True) + NORM_EPS) * w_row


def _norm_kernel(x_ref, w_ref, o_ref):
    o_ref[...] = _rms(x_ref[...], w_ref[...]).astype(o_ref.dtype)


def rmsnorm(x2d, w, out_dtype, tm=512):
    T = x2d.shape[0]
    return pl.pallas_call(
        _norm_kernel,
        grid=(T // tm,),
        in_specs=[pl.BlockSpec((tm, D_MODEL), lambda i: (i, 0)),
                  pl.BlockSpec((1, D_MODEL), lambda i: (0, 0))],
        out_specs=pl.BlockSpec((tm, D_MODEL), lambda i: (i, 0)),
        out_shape=jax.ShapeDtypeStruct((T, D_MODEL), out_dtype),
        compiler_params=_cparams(("parallel",)),
        name="rmsnorm",
    )(x2d, w.reshape(1, D_MODEL))


IN_TN = P_WIDTH // 2
IN_CH = 256


def _inproj_kernel(u_ref, w_ref, o_ref):
    u = u_ref[...]
    for c in range(0, IN_TN, IN_CH):
        o_ref[:, c:c + IN_CH] = jnp.dot(u, w_ref[:, c:c + IN_CH], preferred_element_type=F32)


def inproj(u, w_main, tm=512):
    T = u.shape[0]
    return pl.pallas_call(
        _inproj_kernel,
        grid=(2, T // tm),
        in_specs=[pl.BlockSpec((tm, D_MODEL), lambda j, i: (i, 0)),
                  pl.BlockSpec((D_MODEL, IN_TN), lambda j, i: (0, j))],
        out_specs=pl.BlockSpec((tm, IN_TN), lambda j, i: (i, j)),
        out_shape=jax.ShapeDtypeStruct((T, P_WIDTH), F32),
        compiler_params=_cparams(("arbitrary", "arbitrary")),
        name="inproj",
    )(u, w_main)


def pack_w_in(w_in_l):
    s = _SRC
    segs = [w_in_l[:, s["mg"]:s["mg"] + 3 * D_MODEL],
            w_in_l[:, s["rq"]:s["rq"] + 2048],
            w_in_l[:, s["dqkv"]:s["dqkv"] + 2048],
            w_in_l[:, s["nq"]:s["nq"] + 1536]]
    w_main = jnp.concatenate(segs, axis=1).astype(BF16)
    w_ba = jnp.pad(w_in_l[:, s["db"]:s["db"] + 16], ((0, 0), (0, 112))).astype(BF16)
    return w_main, w_ba


def _rope_tables(L):
    half = RET_DK // 2
    inv = ROPE_BASE ** (-jnp.arange(half, dtype=F32) / half)
    ang = jnp.arange(L, dtype=F32)[:, None] * inv[None, :]
    cos, sin = jnp.cos(ang), jnp.sin(ang)
    return jnp.concatenate([cos, cos], axis=1), jnp.concatenate([-sin, sin], axis=1)


def _rope(x, cosf, sinf):
    return x * cosf + pltpu.roll(x, RET_DK // 2, 1) * sinf


def _row_iota(shape):
    return lax.broadcasted_iota(jnp.int32, shape, 0).astype(F32)


def _run_interleaved(chains):
    live = list(chains)
    while live:
        live = [g for g in live if next(g, StopIteration) is not StopIteration]


def _ret_bwd_kernel(n_seq, lg_ref, q_ref, k_ref, v_ref, cos_ref, sin_ref, o_ref, s_ref):
    C = RET_CHUNK

    @pl.when(pl.program_id(1) == 0)
    def _():
        s_ref[...] = jnp.zeros_like(s_ref)

    cosf, sinf = cos_ref[...], sin_ref[...]
    ri = _row_iota((C, RET_DK))

    def chain(j, h):
        sl = slice(h * RET_DK, (h + 1) * RET_DK)
        lg = lg_ref[1, h]
        q = _rope(q_ref[j, :, sl], cosf, sinf)
        k = _rope(k_ref[j, :, sl], cosf, sinf) * (RET_DK ** -0.5)
        v = v_ref[j, :, sl]
        qd = q * jnp.exp(lg * (C - ri))
        kd = k * jnp.exp(lg * ri)
        S = s_ref[j, h]
        o_ref[j, :, sl] = _mm(qd, S)
        yield
        s_ref[j, h] = S * jnp.exp(lg * C) + _mm(kd.T, v)

    _run_interleaved(chain(j, h) for j in range(n_seq) for h in range(RET_H))


def _ret_fwd_kernel(n_seq, lg_ref, q_ref, k_ref, v_ref, g_ref, ib_ref, cos_ref, sin_ref, gnw_ref, o_ref,
                    s_ref, dm_ref):
    C = RET_CHUNK

    @pl.when(pl.program_id(1) == 0)
    def _():
        s_ref[...] = jnp.zeros_like(s_ref)
        diff = (lax.broadcasted_iota(jnp.int32, (C, C), 0)
                - lax.broadcasted_iota(jnp.int32, (C, C), 1)).astype(F32)
        for h in range(RET_H):
            dm_ref[h] = jnp.where(diff >= 0, jnp.exp(lg_ref[0, h] * jnp.maximum(diff, 0.0)),
                                  jnp.exp(lg_ref[1, h] * jnp.maximum(-diff, 0.0)))

    cosf, sinf = cos_ref[...], sin_ref[...]
    ri = _row_iota((C, RET_DK))

    def chain(j, h):
        sl = slice(h * RET_DK, (h + 1) * RET_DK)
        lg = lg_ref[0, h]
        q = _rope(q_ref[j, :, sl], cosf, sinf)
        k = _rope(k_ref[j, :, sl], cosf, sinf) * (RET_DK ** -0.5)
        v = v_ref[j, :, sl]
        s = _mm_nt(q, k) * dm_ref[h]
        S = s_ref[j, h]
        inter = _mm(q * jnp.exp(lg * (ri + 1.0)), S)
        yield
        r = _mm(s, v) + inter + ib_ref[j, :, sl]
        s_ref[j, h] = S * jnp.exp(lg * C) + _mm((k * jnp.exp(lg * (C - 1.0 - ri))).T, v)
        yield
        mu = jnp.mean(r, axis=-1, keepdims=True)
        rc = r - mu
        var = jnp.mean(rc * rc, axis=-1, keepdims=True)
        rn = rc * lax.rsqrt(var + GN_EPS) * gnw_ref[:, sl]
        o_ref[j, :, sl] = (_silu(g_ref[j, :, sl]) * rn).astype(o_ref.dtype)

    _run_interleaved(chain(j, h) for j in range(n_seq) for h in range(RET_H))


RET_SEQ_PER_STEP = 4


def retention(p, lg, gn_w, cosf, sinf, B, L):
    C = RET_CHUNK
    N = L // C
    T = B * L
    ns = math.gcd(B, RET_SEQ_PER_STEP)
    smem = pl.BlockSpec(memory_space=pltpu.SMEM)
    p4 = p.reshape(B // ns, ns, L, P_WIDTH)

    def pspec(col, rev):
        if rev:
            return pl.BlockSpec((None, ns, C, CB), lambda b, n: (b, 0, N - 1 - n, col))
        return pl.BlockSpec((None, ns, C, CB), lambda b, n: (b, 0, n, col))

    def tspec(rev):
        if rev:
            return pl.BlockSpec((C, RET_DK), lambda b, n: (N - 1 - n, 0))
        return pl.BlockSpec((C, RET_DK), lambda b, n: (n, 0))

    state = pltpu.VMEM((ns, RET_H, RET_DK, RET_DV), F32)
    inter_b = pl.pallas_call(
        functools.partial(_ret_bwd_kernel, ns),
        grid=(B // ns, N),
        in_specs=[smem, pspec(COL_RQ, True), pspec(COL_RK, True), pspec(COL_RV, True),
                  tspec(True), tspec(True)],
        out_specs=pspec(0, True),
        out_shape=jax.ShapeDtypeStruct((B // ns, ns, L, CB), F32),
        scratch_shapes=[state],
        compiler_params=_cparams(("arbitrary", "arbitrary")),
        name="ret_bwd",
    )(lg, p4, p4, p4, cosf, sinf)

    out = pl.pallas_call(
        functools.partial(_ret_fwd_kernel, ns),
        grid=(B // ns, N),
        in_specs=[smem, pspec(COL_RQ, False), pspec(COL_RK, False), pspec(COL_RV, False),
                  pspec(COL_RG, False), pspec(0, False),
                  tspec(False), tspec(False), pl.BlockSpec((1, CB), lambda b, n: (0, 0))],
        out_specs=pspec(0, False),
        out_shape=jax.ShapeDtypeStruct((B // ns, ns, L, CB), BF16),
        scratch_shapes=[state, pltpu.VMEM((RET_H, C, C), F32)],
        compiler_params=_cparams(("arbitrary", "arbitrary")),
        name="ret_fwd",
    )(lg, p4, p4, p4, p4, inter_b, cosf, sinf, gn_w.reshape(1, CB))
    return out.reshape(T, CB)


DN_PREP_TM = 256


def _softplus(x):
    return jnp.maximum(x, 0.0) + jnp.log1p(jnp.exp(-jnp.abs(x)))


def _dn_prep_kernel(n_seq_tiles, u_ref, wba_ref, prm_ref, cw_ref, tri_ref,
                    q_ref, qp_ref, qn_ref, k_ref, kp_ref, kn_ref, v_ref, vp_ref, vn_ref,
                    qo_ref, ko_ref, vo_ref, bg_ref):
    tm = DN_PREP_TM
    i = pl.program_id(0)
    first = (i % n_seq_tiles) == 0
    last = (i % n_seq_tiles) == n_seq_tiles - 1
    rows = lax.broadcasted_iota(jnp.int32, (tm, CB), 0)

    def conv_silu(x_ref, xp_ref, xn_ref, seg):
        x = x_ref[...]
        prev_row = jnp.where(first, 0.0, xp_ref[7:8, :])
        next_row = jnp.where(last, 0.0, xn_ref[0:1, :])
        x_prev = jnp.where(rows == 0, prev_row, pltpu.roll(x, 1, 0))
        x_next = jnp.where(rows == tm - 1, next_row, pltpu.roll(x, tm - 1, 0))
        cs = slice(seg * CB, (seg + 1) * CB)
        y = x_prev * cw_ref[0:1, cs] + x * cw_ref[1:2, cs] + x_next * cw_ref[2:3, cs]
        return _silu(y)

    def l2n(y, scale):
        outs = []
        for h in range(DN_H):
            s = y[:, h * DN_DK:(h + 1) * DN_DK]
            outs.append(s * (lax.rsqrt(jnp.sum(s * s, axis=-1, keepdims=True) + 1e-6) * scale))
        return jnp.concatenate(outs, axis=1)

    qo_ref[...] = l2n(conv_silu(q_ref, qp_ref, qn_ref, 0), DN_DK ** -0.5)
    ko_ref[...] = l2n(conv_silu(k_ref, kp_ref, kn_ref, 1), 1.0)
    vo_ref[...] = conv_silu(v_ref, vp_ref, vn_ref, 2)

    ba = jnp.dot(u_ref[...], wba_ref[...], preferred_element_type=F32)
    lane = lax.broadcasted_iota(jnp.int32, ba.shape, 1)
    beta = _sigmoid(ba)
    g = -prm_ref[0:1, :] * _softplus(ba + prm_ref[1:2, :])
    g_only = jnp.where((lane >= 2 * DN_H) & (lane < 4 * DN_H), g, 0.0)
    cum = jnp.where(lane < 3 * DN_H, _mm_exact_rhs(tri_ref[0], g_only), _mm_exact_rhs(tri_ref[1], g_only))
    tot = _mm_exact_rhs(tri_ref[2], g_only)
    bg_ref[...] = (jnp.where(lane < 2 * DN_H, beta, g_only)
                   + pltpu.roll(cum, 2 * DN_H, 1) + pltpu.roll(tot, 4 * DN_H, 1))


def dn_prep(p, u, w_ba, conv_w, A_log, dt_bias, L):
    T = p.shape[0]
    tm = DN_PREP_TM
    nb8 = T // 8
    prm = jnp.zeros((2, 128), F32)
    prm = prm.at[0, 8:16].set(jnp.exp(A_log.astype(F32)).reshape(-1))
    prm = prm.at[1, 8:16].set(dt_bias.astype(F32).reshape(-1))

    def trio(col):
        return [pl.BlockSpec((tm, CB), lambda i: (i, col)),
                pl.BlockSpec((8, CB), lambda i: (jnp.maximum(i * (tm // 8) - 1, 0), col)),
                pl.BlockSpec((8, CB), lambda i: (jnp.minimum((i + 1) * (tm // 8), nb8 - 1), col))]

    full = lambda shape: pl.BlockSpec(shape, lambda i: (0,) * len(shape))
    out_sd = jax.ShapeDtypeStruct((T, CB), F32)
    assert tm == DN_H * DN_CHUNK
    _, tri = _dn_masks()
    return pl.pallas_call(
        functools.partial(_dn_prep_kernel, L // tm),
        grid=(T // tm,),
        in_specs=[pl.BlockSpec((tm, D_MODEL), lambda i: (i, 0)), full((D_MODEL, 128)), full((2, 128)),
                  full((DN_CONV, 3 * CB)), full(tri.shape)] + trio(COL_DQ) + trio(COL_DK) + trio(COL_DV),
        out_specs=[pl.BlockSpec((tm, CB), lambda i: (i, 0))] * 3 + [pl.BlockSpec((tm, 128), lambda i: (i, 0))],
        out_shape=[out_sd, out_sd, out_sd, jax.ShapeDtypeStruct((T, 128), F32)],
        compiler_params=_cparams(("parallel",)),
        name="dn_prep",
    )(u, w_ba, prm, conv_w.astype(F32), tri, p, p, p, p, p, p, p, p, p)


_M_INCL_F, _M_STRICT_F, _M_INCL_B, _M_STRICT_B, _M_SAME8, _M_LEV8, _M_LEV16, _M_LEV32, _M_EYE, _M_PAIR = range(10)


def _dn_masks():
    import numpy as np
    n = DN_H * DN_CHUNK
    r = np.arange(n)[:, None]
    c = np.arange(n)[None, :]
    same_head = (r // DN_CHUNK) == (c // DN_CHUNK)
    ir, ic = r % DN_CHUNK, c % DN_CHUNK
    lev = lambda s: ((r // (2 * s)) == (c // (2 * s))) & ((r // s) != (c // s))
    ms = [same_head & (ir >= ic), same_head & (ir > ic), same_head & (ir <= ic), same_head & (ir < ic),
          (r // 8) == (c // 8), lev(8), lev(16), lev(32), r == c, (r // 128) == (c // 128)]
    m32 = np.stack(ms).astype(np.float32)
    m16 = np.stack([ms[_M_INCL_F], ms[_M_INCL_B], same_head]).astype(np.float32)
    return jnp.asarray(m32), jnp.asarray(m16, dtype=BF16)


def _hstack(x):
    return jnp.concatenate([x[:, h * DN_DK:(h + 1) * DN_DK] for h in range(DN_H)], axis=0)


def _hunstack(xs):
    return jnp.concatenate([xs[h * DN_CHUNK:(h + 1) * DN_CHUNK] for h in range(DN_H)], axis=1)


def _dn_scan_kernel(n_seq, m_ref, *refs):
    C = DN_CHUNK
    n_chain = 2 * n_seq
    in_refs, out_refs, s_ref = refs[:8], refs[8:10], refs[10]

    @pl.when(pl.program_id(1) == 0)
    def _():
        s_ref[...] = jnp.zeros_like(s_ref)

    eye = m_ref[_M_EYE]
    pair = m_ref[_M_PAIR]

    def chain(ch):
        j, d = ch // 2, ch % 2
        q_ref, k_ref, v_ref, bg_ref = (r.at[j] for r in in_refs[4 * d:4 * d + 4])
        o_ref = out_refs[d].at[j]
        incl = m_ref[_M_INCL_F if d == 0 else _M_INCL_B]
        strict = m_ref[_M_STRICT_F if d == 0 else _M_STRICT_B]
        Qs, Ks, Vs = _hstack(q_ref[...]), _hstack(k_ref[...]), _hstack(v_ref[...])
        bg = bg_ref[...]
        col = lambda j: jnp.concatenate([bg[:, j + h:j + h + 1] for h in range(DN_H)], axis=0)
        beta = col(d * DN_H)
        gc = jnp.broadcast_to(col(4 * DN_H + d * DN_H), (DN_H * C, 128))
        gt = jnp.broadcast_to(col(6 * DN_H + d * DN_H), (DN_H * C, 128))
        gc_row = gc.T[0:1, :]
        diff = jnp.concatenate([gc, gc], axis=1) - gc_row
        decay = jnp.exp(jnp.minimum(diff, 0.0)) * incl
        kb = Ks * beta
        G = _mm_nt(jnp.concatenate([kb, Qs], axis=0), Ks)
        yield
        A = G[:DN_H * C] * (decay * strict)
        attn = G[DN_H * C:] * decay
        P = -(A * m_ref[_M_SAME8])
        X = eye + P
        P = _mm(P, P)
        yield
        X = X + _mm(X, P)
        P = _mm(P, P)
        yield
        X = X + _mm(X, P)
        yield
        for lev in (_M_LEV8, _M_LEV16, _M_LEV32):
            Y = _mm(A * m_ref[lev], X)
            yield
            X = X - _mm(X, Y)
            yield
        egc = jnp.exp(gc)
        sol = _mm(X, jnp.concatenate([Vs * beta, kb * egc], axis=1))
        yield
        u_s, w_s = sol[:, :DN_DV], sol[:, DN_DV:]
        w_l = _hunstack(w_s)
        qd_l = _hunstack(Qs * egc)
        kd_l = _hunstack(Ks * jnp.exp(gt - gc))
        wS, qS = [], []
        for pr in range(2):
            sl = slice(pr * 256, (pr + 1) * 256)
            res = _mm(jnp.concatenate([w_l[:, sl], qd_l[:, sl]], axis=0), s_ref[ch, pr])
            wS.append(res[:C])
            qS.append(res[C:])
        yield
        vnew_s = u_s - _hstack(jnp.concatenate(wS, axis=1))
        o_s = _hstack(jnp.concatenate(qS, axis=1)) + _mm(attn, vnew_s)
        o_ref[...] = _hunstack(o_s)
        vnew_l = _hunstack(vnew_s)
        egt = jnp.exp(gt)
        for pr in range(2):
            sl = slice(pr * 256, (pr + 1) * 256)
            upd = _mm(kd_l[:, sl].T, vnew_l[:, sl])
            ha, hb = egt[(2 * pr) * C:(2 * pr + 1) * C], egt[(2 * pr + 1) * C:(2 * pr + 2) * C]
            cd = jnp.concatenate([ha, ha, hb, hb], axis=0)
            s_ref[ch, pr] = s_ref[ch, pr] * jnp.concatenate([cd, cd], axis=1) + upd * pair

    _run_interleaved(chain(ch) for ch in range(n_chain))


DN_SEQ_PER_STEP = 2


def dn_scan(qn, kn, vs, bg, B, L):
    C = DN_CHUNK
    N = L // C
    T = B * L
    ns = DN_SEQ_PER_STEP
    assert B % ns == 0
    m32, _ = _dn_masks()

    def spec(w, rev):
        if rev:
            return pl.BlockSpec((None, ns, C, w), lambda b, n: (b, 0, N - 1 - n, 0))
        return pl.BlockSpec((None, ns, C, w), lambda b, n: (b, 0, n, 0))

    r4 = lambda a: a.reshape(B // ns, ns, L, a.shape[-1])
    q4, k4, v4, bg4 = r4(qn), r4(kn), r4(vs), r4(bg)
    in_specs = [pl.BlockSpec(m32.shape, lambda b, n: (0, 0, 0))]
    args = [m32]
    for rev in (False, True):
        in_specs += [spec(CB, rev), spec(CB, rev), spec(CB, rev), spec(128, rev)]
        args += [q4, k4, v4, bg4]
    out_sd = jax.ShapeDtypeStruct((B // ns, ns, L, CB), F32)
    of, ob = pl.pallas_call(
        functools.partial(_dn_scan_kernel, ns),
        grid=(B // ns, N),
        in_specs=in_specs,
        out_specs=[spec(CB, False), spec(CB, True)],
        out_shape=[out_sd, out_sd],
        scratch_shapes=[pltpu.VMEM((2 * ns, 2, 256, 256), F32)],
        compiler_params=_cparams(("arbitrary", "arbitrary")),
        name="dn_scan",
    )(*args)
    return of.reshape(T, CB), ob.reshape(T, CB)


def _na_bias_table(rpb):
    qc = jnp.arange(GRID_W)[:, None]
    kc = jnp.arange(GRID_W)[None, :]
    cs = jnp.clip(qc - NA_WIN_C // 2, 0, GRID_W - NA_WIN_C)
    ok = (kc >= cs) & (kc < cs + NA_WIN_C)
    dc = jnp.clip(kc - qc, -(NA_WIN_C - 1), NA_WIN_C - 1) + NA_WIN_C - 1
    delta = jnp.arange(NA_WIN_R)[:, None]
    j = jnp.arange(NA_WIN_R)[None, :]
    dr = j - delta + NA_WIN_R - 1
    t = rpb.astype(F32)[:, dr][:, :, :, dc]
    t = jnp.where(ok[None, None, None], t, NEG_BIG)
    t = t.transpose(1, 0, 3, 2, 4)
    return t.reshape(NA_WIN_R, NA_H, GRID_W, NA_WIN_R * GRID_W)


def _na_kernel(q_ref, k_ref, v_ref, bias_ref, o_ref):
    nk = NA_WIN_R * GRID_W
    q = q_ref[...] * (NA_DH ** -0.5)
    kw = k_ref[...].reshape(nk, CB)
    vw = v_ref[...].reshape(nk, CB)
    sls = [slice(h * NA_DH, (h + 1) * NA_DH) for h in range(NA_H)]
    s = [_mm_nt(q[:, sl], kw[:, sl]) + bias_ref[h] for h, sl in enumerate(sls)]
    e = [jnp.exp(x - jnp.max(x, axis=-1, keepdims=True)) for x in s]
    pv = [_mm(x, vw[:, sl]) for x, sl in zip(e, sls)]
    outs = [x / jnp.sum(y, axis=-1, keepdims=True) for x, y in zip(pv, e)]
    o_ref[...] = jnp.concatenate(outs, axis=1).astype(o_ref.dtype)


def neighborhood_attention(p, bias, B, L):
    rows = L // GRID_W
    assert rows >= NA_WIN_R
    p3 = p.reshape(B * rows, GRID_W, P_WIDTH)

    def rs(r):
        return jnp.clip(r - NA_WIN_R // 2, 0, rows - NA_WIN_R)

    win = lambda col: pl.BlockSpec((pl.Element(NA_WIN_R), pl.Element(GRID_W), pl.Element(CB)),
                                   lambda b, r: (b * rows + rs(r), 0, col * CB))
    out = pl.pallas_call(
        _na_kernel,
        grid=(B, rows),
        in_specs=[pl.BlockSpec((None, GRID_W, CB), lambda b, r: (b * rows + r, 0, COL_NQ)),
                  win(COL_NK), win(COL_NV),
                  pl.BlockSpec((None, NA_H, GRID_W, NA_WIN_R * GRID_W), lambda b, r: (r - rs(r), 0, 0, 0))],
        out_specs=pl.BlockSpec((None, GRID_W, CB), lambda b, r: (b * rows + r, 0, 0)),
        out_shape=jax.ShapeDtypeStruct((B * rows, GRID_W, CB), BF16),
        compiler_params=_cparams(("parallel", "arbitrary")),
        name="natten",
    )(p3, p3, p3, bias)
    return out.reshape(B * L, CB)


MERGE_TM = 256


def _merge_kernel(h_ref, ret_ref, of_ref, ob_ref, dg_ref, na_ref, mg0_ref, mg1_ref, mg2_ref,
                  gb_ref, wb_ref, wo_ref, dnw_ref, nf_ref, wr_ref,
                  ho_ref, uo_ref, aff_ref, afft_ref):
    o = of_ref[...] + ob_ref[...]
    dn = []
    for h in range(DN_H):
        sl = slice(h * DN_DV, (h + 1) * DN_DV)
        oh = o[:, sl]
        dn.append(oh * lax.rsqrt(jnp.mean(oh * oh, axis=-1, keepdims=True) + NORM_EPS))
    dn = jnp.concatenate(dn, axis=1) * dnw_ref[...] * _silu(dg_ref[...])
    branches = (ret_ref[...], dn.astype(BF16), na_ref[...])
    merged = None
    for i, (br, mg_ref) in enumerate(zip(branches, (mg0_ref, mg1_ref, mg2_ref))):
        t = _sigmoid(mg_ref[...] + gb_ref[i:i + 1, :]) * jnp.dot(br, wb_ref[i], preferred_element_type=F32)
        merged = t if merged is None else merged + t
    h_new = h_ref[...] + jnp.dot(merged.astype(BF16), wo_ref[...], preferred_element_type=F32)
    ho_ref[...] = h_new
    u2 = _rms(h_new, nf_ref[...])
    uo_ref[...] = u2
    logits = _mm3(u2, wr_ref[...])
    lane = lax.broadcasted_iota(jnp.int32, logits.shape, 1)
    logits = jnp.where(lane < EC_EXPERTS, logits, NEG_BIG)
    e = jnp.exp(logits - jnp.max(logits, axis=-1, keepdims=True))
    aff = e / jnp.sum(e, axis=-1, keepdims=True)
    aff_ref[...] = aff.T[0:EC_EXPERTS, :]
    afft_ref[...] = aff


def merge(h, ret, of, ob, na, p, gate_b_l, w_branch_l, w_out_l, dn_norm_w_l, norm_ffn_l, w_router_l):
    T = h.shape[0]
    tm = MERGE_TM
    row = lambda w, col=0: pl.BlockSpec((tm, w), lambda i: (i, col))
    full = lambda shape: pl.BlockSpec(shape, lambda i: (0,) * len(shape))
    wr = jnp.pad(w_router_l.astype(F32), ((0, 0), (0, 128 - EC_EXPERTS)))
    return pl.pallas_call(
        _merge_kernel,
        grid=(T // tm,),
        in_specs=[row(D_MODEL), row(CB), row(CB), row(CB), row(CB, COL_DG), row(CB),
                  row(D_MODEL, 0), row(D_MODEL, 1), row(D_MODEL, 2),
                  full((N_BRANCH, D_MODEL)), full((N_BRANCH, BRANCH_W, D_MODEL)), full((D_MODEL, D_MODEL)),
                  full((1, CB)), full((1, D_MODEL)), full((D_MODEL, 128))],
        out_specs=[row(D_MODEL), row(D_MODEL), pl.BlockSpec((EC_EXPERTS, tm), lambda i: (0, i)), row(128)],
        out_shape=[jax.ShapeDtypeStruct((T, D_MODEL), F32), jax.ShapeDtypeStruct((T, D_MODEL), F32),
                   jax.ShapeDtypeStruct((EC_EXPERTS, T), F32), jax.ShapeDtypeStruct((T, 128), F32)],
        compiler_params=_cparams(("parallel",)),
        name="merge",
    )(h, ret, of, ob, p, na, p, p, p,
      gate_b_l.astype(F32), w_branch_l.astype(BF16), w_out_l.astype(BF16),
      jnp.tile(dn_norm_w_l.astype(F32), DN_H).reshape(1, CB), norm_ffn_l.astype(F32).reshape(1, D_MODEL), wr)


FFN_TS = 256


def _ffn_kernel(idx_ref, idxn_ref, x_hbm, wg_ref, wu_ref, wd_ref, y_ref, xbuf, sem):
    ts = FFN_TS
    k = pl.program_id(0)
    nk = pl.num_programs(0)
    slot = k % 2

    def row_copy(src_row, s, r):
        return pltpu.make_async_copy(x_hbm.at[pl.ds(src_row, 1), :], xbuf.at[s, pl.ds(r, 1), :], sem.at[s])

    def issue(iref, s):
        def body(r, c):
            row_copy(iref[0, 0, r], s, r).start()
            return c
        lax.fori_loop(0, ts, body, 0, unroll=8)

    @pl.when(k == 0)
    def _():
        issue(idx_ref, 0)

    @pl.when(k + 1 < nk)
    def _():
        issue(idxn_ref, 1 - slot)

    def wbody(r, c):
        row_copy(0, slot, r).wait()
        return c
    lax.fori_loop(0, ts, wbody, 0, unroll=8)

    x = xbuf[slot].astype(BF16)
    hg = jnp.dot(x, wg_ref[0], preferred_element_type=F32)
    hu = jnp.dot(x, wu_ref[0], preferred_element_type=F32)
    hh = (_silu(hg) * hu).astype(BF16)
    y_ref[...] = jnp.dot(hh, wd_ref[0], preferred_element_type=F32).astype(y_ref.dtype)


def moe_ffn(idx, x, wg, wu, wd):
    E, cap = idx.shape
    ts = FFN_TS
    nt = cap // ts
    K = E * nt
    idx3 = idx.reshape(K, 1, ts).astype(jnp.int32)
    wspec = pl.BlockSpec((1, D_MODEL, D_MODEL), lambda k: (k // nt, 0, 0))
    return pl.pallas_call(
        _ffn_kernel,
        grid=(K,),
        in_specs=[pl.BlockSpec((1, 1, ts), lambda k: (k, 0, 0), memory_space=pltpu.SMEM),
                  pl.BlockSpec((1, 1, ts), lambda k: (jnp.minimum(k + 1, K - 1), 0, 0), memory_space=pltpu.SMEM),
                  pl.BlockSpec(memory_space=pl.ANY),
                  wspec, wspec, wspec],
        out_specs=pl.BlockSpec((ts, D_MODEL), lambda k: (k, 0)),
        out_shape=jax.ShapeDtypeStruct((E * cap, D_MODEL), BF16),
        scratch_shapes=[pltpu.VMEM((2, ts, D_MODEL), F32), pltpu.SemaphoreType.DMA((2,))],
        compiler_params=_cparams(("arbitrary",)),
        name="moe_ffn",
    )(idx3, idx3, x, wg, wu, wd)


ROUTE_SC = 1024


def _route_kernel(cap, aff_ref, pos_ref, offs_ref, idx_ref):
    R = aff_ref.shape[0]
    a = aff_ref[...]
    keys = pltpu.bitcast(a, jnp.int32)

    def count(mask):
        return jnp.sum(jnp.where(mask, 1.0, 0.0))

    def search(i, thr):
        cand = thr | (jnp.int32(1) << (30 - i))
        return jnp.where(count(keys >= cand) >= cap, cand, thr)

    thr = lax.fori_loop(0, 31, search, jnp.int32(0))
    above = keys > thr
    tie = keys == thr
    need = cap - count(above)

    li = lax.broadcasted_iota(jnp.int32, (128, 128), 0)
    lj = lax.broadcasted_iota(jnp.int32, (128, 128), 1)
    upper = jnp.where(li <= lj, 1.0, 0.0).astype(BF16)
    ri = lax.broadcasted_iota(jnp.int32, (R, R), 0)
    rj = lax.broadcasted_iota(jnp.int32, (R, R), 1)
    lower_strict = jnp.where(rj < ri, 1.0, 0.0).astype(BF16)
    dot = functools.partial(jnp.dot, preferred_element_type=F32)

    def prefix(m):
        pin = dot(m.astype(BF16), upper)
        tot = jnp.broadcast_to(pin[:, 127:128], (R, 128))
        offs = dot(lower_strict, tot.astype(BF16))
        return pin, tot, offs

    tie_f = jnp.where(tie, 1.0, 0.0)
    pin, _, offs = prefix(tie_f)
    sel = above | (tie & ((offs + pin - tie_f) < need))
    sel_f = jnp.where(sel, 1.0, 0.0)
    pin, tot, offs = prefix(sel_f)
    pos_ref[...] = jnp.where(sel, offs + pin - sel_f, -1.0).astype(jnp.int32)
    offs_ref[...] = offs.astype(jnp.int32)

    rinc_row = (offs + tot).T[0:1, :]
    rhs = jnp.concatenate([jnp.ones((R, 128), BF16), tot.astype(BF16)], axis=1)
    pin_b = pin.astype(BF16)
    ones128 = jnp.ones((128, 128), BF16)
    SC = ROUTE_SC
    lane_r = lax.broadcasted_iota(jnp.int32, (SC, R), 1).astype(F32)
    for c in range(cap // SC):
        s_col = (lax.broadcasted_iota(jnp.int32, (SC, R), 0) + c * SC).astype(F32)
        before = jnp.where(rinc_row <= s_col, 1.0, 0.0).astype(BF16)
        rb = dot(before, rhs)
        row, base = rb[:, :128], rb[:, 128:]
        row_r = jnp.concatenate([row] * (R // 128), axis=1)
        onehot = jnp.where(lane_r == row_r, 1.0, 0.0).astype(BF16)
        pin_row = dot(onehot, pin_b)
        k = s_col[:, :128] - base
        lane = dot(jnp.where(pin_row <= k, 1.0, 0.0).astype(BF16), ones128)
        idx = row * 128.0 + lane
        idx_ref[:, c * SC:(c + 1) * SC] = idx.T[0:8, :].astype(jnp.int32)


def route(aff_t, cap):
    E, T = aff_t.shape
    R = T // 128
    assert cap % ROUTE_SC == 0 and R % 128 == 0
    blk = pl.BlockSpec((None, R, 128), lambda e: (e, 0, 0))
    pos, offs, idx = pl.pallas_call(
        functools.partial(_route_kernel, cap),
        grid=(E,),
        in_specs=[blk],
        out_specs=[blk, blk, pl.BlockSpec((None, 8, cap), lambda e: (e, 0, 0))],
        out_shape=[jax.ShapeDtypeStruct((E, R, 128), jnp.int32), jax.ShapeDtypeStruct((E, R, 128), jnp.int32),
                   jax.ShapeDtypeStruct((E, 8, cap), jnp.int32)],
        compiler_params=_cparams(("parallel",)),
        name="route",
    )(aff_t.reshape(E, R, 128))
    return pos.reshape(E, T), offs[:, :, 0], idx[:, 0, :]


CMB_TM = 512
CMB_W = 128
CMB_ALIGN = 16


def _combine_kernel(emit_h, n_rows, start_ref, nwin_ref, h_ref, pos_ref, aff_ref, w_ref, y_hbm, *rest):
    out_refs, (ybuf, yov, acc, sem, sem_ov) = rest[:-5], rest[-5:]
    E, W, tm = EC_EXPERTS, CMB_W, CMB_TM
    i = pl.program_id(0)
    nt = pl.num_programs(0)
    slot = i % 2

    def win_copy(tile, e, s):
        return pltpu.make_async_copy(y_hbm.at[pl.ds(pl.multiple_of(start_ref[tile * E + e], CMB_ALIGN), W), :],
                                     ybuf.at[s, pl.ds(e * W, W), :], sem.at[s])

    def issue(tile, s):
        for e in range(E):
            win_copy(tile, e, s).start()

    @pl.when(i == 0)
    def _():
        issue(0, 0)

    @pl.when(i + 1 < nt)
    def _():
        issue(i + 1, 1 - slot)

    for e in range(E):
        win_copy(i, e, slot).wait()

    pos = pos_ref[...]
    aff = aff_ref[...]
    lane = lax.broadcasted_iota(jnp.int32, (tm, W), 1)

    def expand(base_of, lo_of=None):
        cols = []
        for e in range(E):
            pe = pos[:, e:e + 1]
            hit = pe - base_of(e) == lane
            if lo_of is not None:
                hit = hit & (pe >= lo_of(e))
            cols.append(jnp.where(hit, aff[:, e:e + 1], 0.0).astype(BF16))
        return jnp.concatenate(cols, axis=1)

    acc[...] = jnp.dot(expand(lambda e: start_ref[i * E + e]), ybuf[slot], preferred_element_type=F32)

    for w in range(1, tm // W + 1):
        @pl.when(nwin_ref[i] > w)
        def _():
            def ov_start(e):
                return pl.multiple_of(jnp.minimum(start_ref[i * E + e] + w * W, n_rows - W), CMB_ALIGN)

            def ov_copy(e):
                return pltpu.make_async_copy(y_hbm.at[pl.ds(ov_start(e), W), :], yov.at[pl.ds(e * W, W), :], sem_ov)
            for e in range(E):
                ov_copy(e).start()
            for e in range(E):
                ov_copy(e).wait()
            acc[...] += jnp.dot(expand(ov_start, lambda e: start_ref[i * E + e] + w * W), yov[...],
                                preferred_element_type=F32)

    h_new = h_ref[...] + acc[...]
    if emit_h:
        out_refs[0][...] = h_new
    out_refs[-1][...] = _rms(h_new, w_ref[...]).astype(out_refs[-1].dtype)


def combine(h, y, pos, offs, aff_tok, cap, w, emit_h, norm_dtype):
    T = h.shape[0]
    E, tm, W = EC_EXPERTS, CMB_TM, CMB_W
    nt = T // tm
    n_rows = E * cap
    ebase = (jnp.arange(E, dtype=jnp.int32) * cap)[:, None]
    pos_tok = jnp.where(pos >= 0, pos + ebase, -1).T
    first = offs[:, ::tm // 128] + ebase
    last = jnp.concatenate([first[:, 1:], ebase + cap], axis=1)
    start_al = (first // CMB_ALIGN) * CMB_ALIGN
    nwin = jnp.max((last - start_al + W - 1) // W, axis=0).astype(jnp.int32)
    start = jnp.minimum(start_al, n_rows - W).T.reshape(-1).astype(jnp.int32)
    row = lambda wd: pl.BlockSpec((tm, wd), lambda i, *_: (i, 0))
    outs = ([jax.ShapeDtypeStruct((T, D_MODEL), F32)] if emit_h else []) + [jax.ShapeDtypeStruct((T, D_MODEL), norm_dtype)]
    return pl.pallas_call(
        functools.partial(_combine_kernel, emit_h, n_rows),
        grid_spec=pltpu.PrefetchScalarGridSpec(
            num_scalar_prefetch=2,
            grid=(nt,),
            in_specs=[row(D_MODEL), row(E), row(128), pl.BlockSpec((1, D_MODEL), lambda i, *_: (0, 0)),
                      pl.BlockSpec(memory_space=pl.ANY)],
            out_specs=[row(D_MODEL)] * len(outs),
            scratch_shapes=[pltpu.VMEM((2, E * W, D_MODEL), BF16), pltpu.VMEM((E * W, D_MODEL), BF16),
                            pltpu.VMEM((tm, D_MODEL), F32), pltpu.SemaphoreType.DMA((2,)),
                            pltpu.SemaphoreType.DMA(())]),
        out_shape=outs,
        compiler_params=_cparams(("arbitrary",)),
        name="combine",
    )(start, nwin, h, pos_tok, aff_tok, w.astype(F32).reshape(1, D_MODEL), y)


def _trunk(x, prm):
    B, L, _ = x.shape
    T = B * L
    cap = EC_CAPACITY * T // EC_EXPERTS
    h = x.reshape(T, D_MODEL)
    cosf, sinf = _rope_tables(L)
    u = rmsnorm(h, prm["norm_mix"][0].astype(F32), BF16)
    for l in range(DEPTH):
        p = inproj(u, prm["w_main"][l])
        ret = retention(p, prm["ret_lg"][l], prm["ret_gn_w"][l].astype(F32), cosf, sinf, B, L)
        qn, kn, vs, bg = dn_prep(p, u, prm["w_ba"][l], prm["dn_conv_w"][l], prm["dn_A_log"][l],
                                 prm["dn_dt_bias"][l], L)
        of, ob = dn_scan(qn, kn, vs, bg, B, L)
        na = neighborhood_attention(p, prm["na_bias"][l], B, L)
        h, u_ffn, aff_t, aff_tok = merge(h, ret, of, ob, na, p, prm["gate_b"][l], prm["w_branch"][l],
                                         prm["w_out"][l], prm["dn_norm_w"][l], prm["norm_ffn"][l],
                                         prm["w_router"][l])
        pos, offs, idx = route(aff_t, cap)
        y = moe_ffn(idx, u_ffn, prm["w_gate_e"][l], prm["w_up_e"][l], prm["w_down_e"][l])
        if l + 1 < DEPTH:
            h, u = combine(h, y, pos, offs, aff_tok, cap, prm["norm_mix"][l + 1], True, BF16)
        else:
            (out,) = combine(h, y, pos, offs, aff_tok, cap, prm["final_norm"], False, F32)
    return out.reshape(B, L, D_MODEL)


def kernel(x_prompt, x_sample, norm_mix, w_in, gate_b, ret_decay, ret_gn_w, dn_conv_w, dn_A_log, dn_dt_bias, dn_norm_w, na_rpb, w_branch, w_out, norm_ffn, w_router, w_gate_e, w_up_e, w_down_e, final_norm):
    packed = [pack_w_in(w_in[l]) for l in range(DEPTH)]
    prm = dict(
        norm_mix=norm_mix, gate_b=gate_b, ret_gn_w=ret_gn_w, dn_conv_w=dn_conv_w, dn_A_log=dn_A_log,
        dn_dt_bias=dn_dt_bias, dn_norm_w=dn_norm_w, w_branch=w_branch, w_out=w_out, norm_ffn=norm_ffn,
        w_router=w_router, final_norm=final_norm,
        w_main=[pk[0] for pk in packed], w_ba=[pk[1] for pk in packed],
        ret_lg=jax.nn.log_sigmoid(ret_decay.astype(F32)),
        na_bias=[_na_bias_table(na_rpb[l]) for l in range(DEPTH)],
        w_gate_e=w_gate_e.astype(BF16), w_up_e=w_up_e.astype(BF16), w_down_e=w_down_e.astype(BF16),
    )
    return (_trunk(x_prompt, prm), _trunk(x_sample, prm))
```

```python
import functools
import math

import jax
import jax.numpy as jnp
from jax import lax
from jax.experimental import pallas as pl
from jax.experimental.pallas import tpu as pltpu

F32 = jnp.float32
BF16 = jnp.bfloat16

D_MODEL = 1024
DEPTH = 4
GRID_W = 64
RET_H, RET_DK, RET_DV, RET_CHUNK = 4, 128, 128, 128
ROPE_BASE = 10000.0
DN_H, DN_DK, DN_DV, DN_CONV, DN_CHUNK = 4, 128, 128, 3, 64
NA_H, NA_DH, NA_WIN_R, NA_WIN_C = 8, 64, 8, 16
N_BRANCH = 3
BRANCH_W = 512
EC_EXPERTS, EC_CAPACITY = 16, 2
NORM_EPS = 1e-6
GN_EPS = 1e-5
NEG_BIG = -1e30

CB = 512
COL_MG, COL_RQ, COL_RK, COL_RV, COL_RG = 0, 6, 7, 8, 9
COL_DQ, COL_DK, COL_DV, COL_DG = 10, 11, 12, 13
COL_NQ, COL_NK, COL_NV = 14, 15, 16
P_WIDTH = 17 * CB
_SRC = dict(rq=0, rk=512, rv=1024, rg=1536, dqkv=2048, dg=3584, db=4096, da=4104,
            nq=4112, nk=4624, nv=5136, mg=5648)

V7X_VMEM_LIMIT = 56 * 1024 * 1024


def _cparams(sem):
    return pltpu.CompilerParams(dimension_semantics=sem, vmem_limit_bytes=V7X_VMEM_LIMIT)


def _mm(a, b):
    return jnp.dot(a.astype(BF16), b.astype(BF16), preferred_element_type=F32)


def _mm_nt(a, b):
    return lax.dot_general(a.astype(BF16), b.astype(BF16), (((1,), (1,)), ((), ())),
                           preferred_element_type=F32)


def _split3(x):
    hi = x.astype(BF16)
    r = x - hi.astype(F32)
    mid = r.astype(BF16)
    lo = (r - mid.astype(F32)).astype(BF16)
    return hi, mid, lo


def _mm_exact_rhs(a_bf16, x):
    hi, mid, lo = _split3(x)
    d = functools.partial(jnp.dot, preferred_element_type=F32)
    return d(a_bf16, hi) + d(a_bf16, mid) + d(a_bf16, lo)


def _mm3(a, b):
    ah = a.astype(BF16)
    al = (a - ah.astype(F32)).astype(BF16)
    bh = b.astype(BF16)
    bl = (b - bh.astype(F32)).astype(BF16)
    d = functools.partial(jnp.dot, preferred_element_type=F32)
    return d(ah, bh) + (d(ah, bl) + d(al, bh))


def _sigmoid(x):
    return 1.0 / (1.0 + jnp.exp(-x))


def _silu(x):
    return x * _sigmoid(x)


def _rms(x, w_row):
    return x * lax.rsqrt(jnp.mean(x * x, axis=-1, keepdims=True) + NORM_EPS) * w_row


def _norm_kernel(x_ref, w_ref, o_ref):
    o_ref[...] = _rms(x_ref[...], w_ref[...]).astype(o_ref.dtype)


def rmsnorm(x2d, w, out_dtype, tm=512):
    T = x2d.shape[0]
    return pl.pallas_call(
        _norm_kernel,
        grid=(T // tm,),
        in_specs=[pl.BlockSpec((tm, D_MODEL), lambda i: (i, 0)),
                  pl.BlockSpec((1, D_MODEL), lambda i: (0, 0))],
        out_specs=pl.BlockSpec((tm, D_MODEL), lambda i: (i, 0)),
        out_shape=jax.ShapeDtypeStruct((T, D_MODEL), out_dtype),
        compiler_params=_cparams(("parallel",)),
        name="rmsnorm",
    )(x2d, w.reshape(1, D_MODEL))


IN_TN = P_WIDTH // 2
IN_CH = 256


def _inproj_kernel(u_ref, w_ref, o_ref):
    u = u_ref[...]
    for c in range(0, IN_TN, IN_CH):
        o_ref[:, c:c + IN_CH] = jnp.dot(u, w_ref[:, c:c + IN_CH],
                                        preferred_element_type=F32).astype(o_ref.dtype)


def inproj(u, w_main, tm=512):
    T = u.shape[0]
    return pl.pallas_call(
        _inproj_kernel,
        grid=(2, T // tm),
        in_specs=[pl.BlockSpec((tm, D_MODEL), lambda j, i: (i, 0)),
                  pl.BlockSpec((D_MODEL, IN_TN), lambda j, i: (0, j))],
        out_specs=pl.BlockSpec((tm, IN_TN), lambda j, i: (i, j)),
        out_shape=jax.ShapeDtypeStruct((T, P_WIDTH), BF16),
        compiler_params=_cparams(("arbitrary", "arbitrary")),
        name="inproj",
    )(u, w_main)


def pack_w_in(w_in_l):
    s = _SRC
    segs = [w_in_l[:, s["mg"]:s["mg"] + 3 * D_MODEL],
            w_in_l[:, s["rq"]:s["rq"] + 2048],
            w_in_l[:, s["dqkv"]:s["dqkv"] + 2048],
            w_in_l[:, s["nq"]:s["nq"] + 1536]]
    w_main = jnp.concatenate(segs, axis=1).astype(BF16)
    w_ba = jnp.pad(w_in_l[:, s["db"]:s["db"] + 16], ((0, 0), (0, 112))).astype(BF16)
    return w_main, w_ba


def _rope_tables(L):
    half = RET_DK // 2
    inv = ROPE_BASE ** (-jnp.arange(half, dtype=F32) / half)
    ang = jnp.arange(L, dtype=F32)[:, None] * inv[None, :]
    cos, sin = jnp.cos(ang), jnp.sin(ang)
    return jnp.concatenate([cos, cos], axis=1), jnp.concatenate([-sin, sin], axis=1)


def _rope(x, cosf, sinf):
    return x * cosf + pltpu.roll(x, RET_DK // 2, 1) * sinf


def _row_iota(shape):
    return lax.broadcasted_iota(jnp.int32, shape, 0).astype(F32)


def _run_interleaved(chains):
    live = list(chains)
    while live:
        live = [g for g in live if next(g, StopIteration) is not StopIteration]


def _ret_bwd_kernel(n_seq, lg_ref, q_ref, k_ref, v_ref, cos_ref, sin_ref, o_ref, s_ref):
    C = RET_CHUNK

    @pl.when(pl.program_id(1) == 0)
    def _():
        s_ref[...] = jnp.zeros_like(s_ref)

    cosf, sinf = cos_ref[...], sin_ref[...]
    ri = _row_iota((C, RET_DK))

    def chain(j, h):
        sl = slice(h * RET_DK, (h + 1) * RET_DK)
        lg = lg_ref[1, h]
        q = _rope(q_ref[j, :, sl].astype(F32), cosf, sinf)
        k = _rope(k_ref[j, :, sl].astype(F32), cosf, sinf) * (RET_DK ** -0.5)
        v = v_ref[j, :, sl]
        qd = q * jnp.exp(lg * (C - ri))
        kd = k * jnp.exp(lg * ri)
        S = s_ref[j, h]
        o_ref[j, :, sl] = _mm(qd, S)
        yield
        s_ref[j, h] = S * jnp.exp(lg * C) + _mm(kd.T, v)

    _run_interleaved(chain(j, h) for j in range(n_seq) for h in range(RET_H))


def _ret_fwd_kernel(n_seq, lg_ref, q_ref, k_ref, v_ref, g_ref, ib_ref, cos_ref, sin_ref, gnw_ref, o_ref,
                    s_ref, dm_ref):
    C = RET_CHUNK

    @pl.when(pl.program_id(1) == 0)
    def _():
        s_ref[...] = jnp.zeros_like(s_ref)
        diff = (lax.broadcasted_iota(jnp.int32, (C, C), 0)
                - lax.broadcasted_iota(jnp.int32, (C, C), 1)).astype(F32)
        for h in range(RET_H):
            dm_ref[h] = jnp.where(diff >= 0, jnp.exp(lg_ref[0, h] * jnp.maximum(diff, 0.0)),
                                  jnp.exp(lg_ref[1, h] * jnp.maximum(-diff, 0.0)))

    cosf, sinf = cos_ref[...], sin_ref[...]
    ri = _row_iota((C, RET_DK))

    def chain(j, h):
        sl = slice(h * RET_DK, (h + 1) * RET_DK)
        lg = lg_ref[0, h]
        q = _rope(q_ref[j, :, sl].astype(F32), cosf, sinf)
        k = _rope(k_ref[j, :, sl].astype(F32), cosf, sinf) * (RET_DK ** -0.5)
        v = v_ref[j, :, sl]
        s = _mm_nt(q, k) * dm_ref[h]
        S = s_ref[j, h]
        inter = _mm(q * jnp.exp(lg * (ri + 1.0)), S)
        yield
        r = _mm(s, v) + inter + ib_ref[j, :, sl]
        s_ref[j, h] = S * jnp.exp(lg * C) + _mm((k * jnp.exp(lg * (C - 1.0 - ri))).T, v)
        yield
        mu = jnp.mean(r, axis=-1, keepdims=True)
        rc = r - mu
        var = jnp.mean(rc * rc, axis=-1, keepdims=True)
        rn = rc * lax.rsqrt(var + GN_EPS) * gnw_ref[:, sl]
        o_ref[j, :, sl] = (_silu(g_ref[j, :, sl].astype(F32)) * rn).astype(o_ref.dtype)

    _run_interleaved(chain(j, h) for j in range(n_seq) for h in range(RET_H))


RET_SEQ_PER_STEP = 4


def retention(p, lg, gn_w, cosf, sinf, B, L):
    C = RET_CHUNK
    N = L // C
    T = B * L
    ns = math.gcd(B, RET_SEQ_PER_STEP)
    smem = pl.BlockSpec(memory_space=pltpu.SMEM)
    p4 = p.reshape(B // ns, ns, L, P_WIDTH)

    def pspec(col, rev):
        if rev:
            return pl.BlockSpec((None, ns, C, CB), lambda b, n: (b, 0, N - 1 - n, col))
        return pl.BlockSpec((None, ns, C, CB), lambda b, n: (b, 0, n, col))

    def tspec(rev):
        if rev:
            return pl.BlockSpec((C, RET_DK), lambda b, n: (N - 1 - n, 0))
        return pl.BlockSpec((C, RET_DK), lambda b, n: (n, 0))

    state = pltpu.VMEM((ns, RET_H, RET_DK, RET_DV), F32)
    inter_b = pl.pallas_call(
        functools.partial(_ret_bwd_kernel, ns),
        grid=(B // ns, N),
        in_specs=[smem, pspec(COL_RQ, True), pspec(COL_RK, True), pspec(COL_RV, True),
                  tspec(True), tspec(True)],
        out_specs=pspec(0, True),
        out_shape=jax.ShapeDtypeStruct((B // ns, ns, L, CB), F32),
        scratch_shapes=[state],
        compiler_params=_cparams(("arbitrary", "arbitrary")),
        name="ret_bwd",
    )(lg, p4, p4, p4, cosf, sinf)

    out = pl.pallas_call(
        functools.partial(_ret_fwd_kernel, ns),
        grid=(B // ns, N),
        in_specs=[smem, pspec(COL_RQ, False), pspec(COL_RK, False), pspec(COL_RV, False),
                  pspec(COL_RG, False), pspec(0, False),
                  tspec(False), tspec(False), pl.BlockSpec((1, CB), lambda b, n: (0, 0))],
        out_specs=pspec(0, False),
        out_shape=jax.ShapeDtypeStruct((B // ns, ns, L, CB), BF16),
        scratch_shapes=[state, pltpu.VMEM((RET_H, C, C), F32)],
        compiler_params=_cparams(("arbitrary", "arbitrary")),
        name="ret_fwd",
    )(lg, p4, p4, p4, p4, inter_b, cosf, sinf, gn_w.reshape(1, CB))
    return out.reshape(T, CB)


DN_PREP_TM = 256
DN_HALO = 16


def _softplus(x):
    return jnp.maximum(x, 0.0) + jnp.log1p(jnp.exp(-jnp.abs(x)))


def _dn_prep_kernel(n_seq_tiles, u_ref, wba_ref, prm_ref, cw_ref, tri_ref,
                    q_ref, qp_ref, qn_ref, k_ref, kp_ref, kn_ref, v_ref, vp_ref, vn_ref,
                    qo_ref, ko_ref, vo_ref, bg_ref):
    tm = DN_PREP_TM
    i = pl.program_id(0)
    first = (i % n_seq_tiles) == 0
    last = (i % n_seq_tiles) == n_seq_tiles - 1
    rows = lax.broadcasted_iota(jnp.int32, (tm, CB), 0)

    def conv_silu(x_ref, xp_ref, xn_ref, seg):
        x = x_ref[...].astype(F32)
        prev_row = jnp.where(first, 0.0, xp_ref[DN_HALO - 1:DN_HALO, :].astype(F32))
        next_row = jnp.where(last, 0.0, xn_ref[0:1, :].astype(F32))
        x_prev = jnp.where(rows == 0, prev_row, pltpu.roll(x, 1, 0))
        x_next = jnp.where(rows == tm - 1, next_row, pltpu.roll(x, tm - 1, 0))
        cs = slice(seg * CB, (seg + 1) * CB)
        y = x_prev * cw_ref[0:1, cs] + x * cw_ref[1:2, cs] + x_next * cw_ref[2:3, cs]
        return _silu(y)

    def l2n(y, scale):
        outs = []
        for h in range(DN_H):
            s = y[:, h * DN_DK:(h + 1) * DN_DK]
            outs.append(s * (lax.rsqrt(jnp.sum(s * s, axis=-1, keepdims=True) + 1e-6) * scale))
        return jnp.concatenate(outs, axis=1)

    qo_ref[...] = l2n(conv_silu(q_ref, qp_ref, qn_ref, 0), DN_DK ** -0.5)
    ko_ref[...] = l2n(conv_silu(k_ref, kp_ref, kn_ref, 1), 1.0)
    vo_ref[...] = conv_silu(v_ref, vp_ref, vn_ref, 2)

    ba = jnp.dot(u_ref[...], wba_ref[...], preferred_element_type=F32)
    lane = lax.broadcasted_iota(jnp.int32, ba.shape, 1)
    beta = _sigmoid(ba)
    g = -prm_ref[0:1, :] * _softplus(ba + prm_ref[1:2, :])
    g_only = jnp.where((lane >= 2 * DN_H) & (lane < 4 * DN_H), g, 0.0)
    cum = jnp.where(lane < 3 * DN_H, _mm_exact_rhs(tri_ref[0], g_only), _mm_exact_rhs(tri_ref[1], g_only))
    tot = _mm_exact_rhs(tri_ref[2], g_only)
    bg_ref[...] = (jnp.where(lane < 2 * DN_H, beta, g_only)
                   + pltpu.roll(cum, 2 * DN_H, 1) + pltpu.roll(tot, 4 * DN_H, 1))


def dn_prep(p, u, w_ba, conv_w, A_log, dt_bias, L):
    T = p.shape[0]
    tm = DN_PREP_TM
    nb8 = T // DN_HALO
    prm = jnp.zeros((2, 128), F32)
    prm = prm.at[0, 8:16].set(jnp.exp(A_log.astype(F32)).reshape(-1))
    prm = prm.at[1, 8:16].set(dt_bias.astype(F32).reshape(-1))

    def trio(col):
        return [pl.BlockSpec((tm, CB), lambda i: (i, col)),
                pl.BlockSpec((DN_HALO, CB), lambda i: (jnp.maximum(i * (tm // DN_HALO) - 1, 0), col)),
                pl.BlockSpec((DN_HALO, CB), lambda i: (jnp.minimum((i + 1) * (tm // DN_HALO), nb8 - 1), col))]

    full = lambda shape: pl.BlockSpec(shape, lambda i: (0,) * len(shape))
    out_sd = jax.ShapeDtypeStruct((T, CB), F32)
    assert tm == DN_H * DN_CHUNK
    _, tri = _dn_masks()
    return pl.pallas_call(
        functools.partial(_dn_prep_kernel, L // tm),
        grid=(T // tm,),
        in_specs=[pl.BlockSpec((tm, D_MODEL), lambda i: (i, 0)), full((D_MODEL, 128)), full((2, 128)),
                  full((DN_CONV, 3 * CB)), full(tri.shape)] + trio(COL_DQ) + trio(COL_DK) + trio(COL_DV),
        out_specs=[pl.BlockSpec((tm, CB), lambda i: (i, 0))] * 3 + [pl.BlockSpec((tm, 128), lambda i: (i, 0))],
        out_shape=[out_sd, out_sd, out_sd, jax.ShapeDtypeStruct((T, 128), F32)],
        compiler_params=_cparams(("parallel",)),
        name="dn_prep",
    )(u, w_ba, prm, conv_w.astype(F32), tri, p, p, p, p, p, p, p, p, p)


_M_INCL_F, _M_STRICT_F, _M_INCL_B, _M_STRICT_B, _M_SAME8, _M_LEV8, _M_LEV16, _M_LEV32, _M_EYE, _M_PAIR = range(10)


def _dn_masks():
    import numpy as np
    n = DN_H * DN_CHUNK
    r = np.arange(n)[:, None]
    c = np.arange(n)[None, :]
    same_head = (r // DN_CHUNK) == (c // DN_CHUNK)
    ir, ic = r % DN_CHUNK, c % DN_CHUNK
    lev = lambda s: ((r // (2 * s)) == (c // (2 * s))) & ((r // s) != (c // s))
    ms = [same_head & (ir >= ic), same_head & (ir > ic), same_head & (ir <= ic), same_head & (ir < ic),
          (r // 8) == (c // 8), lev(8), lev(16), lev(32), r == c, (r // 128) == (c // 128)]
    m32 = np.stack(ms).astype(np.float32)
    m16 = np.stack([ms[_M_INCL_F], ms[_M_INCL_B], same_head]).astype(np.float32)
    return jnp.asarray(m32), jnp.asarray(m16, dtype=BF16)


def _hstack(x):
    return jnp.concatenate([x[:, h * DN_DK:(h + 1) * DN_DK] for h in range(DN_H)], axis=0)


def _hunstack(xs):
    return jnp.concatenate([xs[h * DN_CHUNK:(h + 1) * DN_CHUNK] for h in range(DN_H)], axis=1)


def _dn_scan_kernel(n_seq, m_ref, *refs):
    C = DN_CHUNK
    n_chain = 2 * n_seq
    in_refs, out_refs, s_ref = refs[:8], refs[8:10], refs[10]

    @pl.when(pl.program_id(1) == 0)
    def _():
        s_ref[...] = jnp.zeros_like(s_ref)

    eye = m_ref[_M_EYE]
    pair = m_ref[_M_PAIR]

    def chain(ch):
        j, d = ch // 2, ch % 2
        q_ref, k_ref, v_ref, bg_ref = (r.at[j] for r in in_refs[4 * d:4 * d + 4])
        o_ref = out_refs[d].at[j]
        incl = m_ref[_M_INCL_F if d == 0 else _M_INCL_B]
        strict = m_ref[_M_STRICT_F if d == 0 else _M_STRICT_B]
        Qs, Ks, Vs = _hstack(q_ref[...]), _hstack(k_ref[...]), _hstack(v_ref[...])
        bg = bg_ref[...]
        col = lambda j: jnp.concatenate([bg[:, j + h:j + h + 1] for h in range(DN_H)], axis=0)
        beta = col(d * DN_H)
        gc = jnp.broadcast_to(col(4 * DN_H + d * DN_H), (DN_H * C, 128))
        gt = jnp.broadcast_to(col(6 * DN_H + d * DN_H), (DN_H * C, 128))
        gc_row = gc.T[0:1, :]
        diff = jnp.concatenate([gc, gc], axis=1) - gc_row
        decay = jnp.exp(jnp.minimum(diff, 0.0)) * incl
        kb = Ks * beta
        G = _mm_nt(jnp.concatenate([kb, Qs], axis=0), Ks)
        yield
        A = G[:DN_H * C] * (decay * strict)
        attn = G[DN_H * C:] * decay
        P = -(A * m_ref[_M_SAME8])
        X = eye + P
        P = _mm(P, P)
        yield
        X = X + _mm(X, P)
        P = _mm(P, P)
        yield
        X = X + _mm(X, P)
        yield
        for lev in (_M_LEV8, _M_LEV16, _M_LEV32):
            Y = _mm(A * m_ref[lev], X)
            yield
            X = X - _mm(X, Y)
            yield
        egc = jnp.exp(gc)
        sol = _mm(X, jnp.concatenate([Vs * beta, kb * egc], axis=1))
        yield
        u_s, w_s = sol[:, :DN_DV], sol[:, DN_DV:]
        w_l = _hunstack(w_s)
        qd_l = _hunstack(Qs * egc)
        kd_l = _hunstack(Ks * jnp.exp(gt - gc))
        wS, qS = [], []
        for pr in range(2):
            sl = slice(pr * 256, (pr + 1) * 256)
            res = _mm(jnp.concatenate([w_l[:, sl], qd_l[:, sl]], axis=0), s_ref[ch, pr])
            wS.append(res[:C])
            qS.append(res[C:])
        yield
        vnew_s = u_s - _hstack(jnp.concatenate(wS, axis=1))
        o_s = _hstack(jnp.concatenate(qS, axis=1)) + _mm(attn, vnew_s)
        o_ref[...] = _hunstack(o_s)
        vnew_l = _hunstack(vnew_s)
        egt = jnp.exp(gt)
        for pr in range(2):
            sl = slice(pr * 256, (pr + 1) * 256)
            upd = _mm(kd_l[:, sl].T, vnew_l[:, sl])
            ha, hb = egt[(2 * pr) * C:(2 * pr + 1) * C], egt[(2 * pr + 1) * C:(2 * pr + 2) * C]
            cd = jnp.concatenate([ha, ha, hb, hb], axis=0)
            s_ref[ch, pr] = s_ref[ch, pr] * jnp.concatenate([cd, cd], axis=1) + upd * pair

    _run_interleaved(chain(ch) for ch in range(n_chain))


DN_SEQ_PER_STEP = 2


def dn_scan(qn, kn, vs, bg, B, L):
    C = DN_CHUNK
    N = L // C
    T = B * L
    ns = DN_SEQ_PER_STEP
    assert B % ns == 0
    m32, _ = _dn_masks()

    def spec(w, rev):
        if rev:
            return pl.BlockSpec((None, ns, C, w), lambda b, n: (b, 0, N - 1 - n, 0))
        return pl.BlockSpec((None, ns, C, w), lambda b, n: (b, 0, n, 0))

    r4 = lambda a: a.reshape(B // ns, ns, L, a.shape[-1])
    q4, k4, v4, bg4 = r4(qn), r4(kn), r4(vs), r4(bg)
    in_specs = [pl.BlockSpec(m32.shape, lambda b, n: (0, 0, 0))]
    args = [m32]
    for rev in (False, True):
        in_specs += [spec(CB, rev), spec(CB, rev), spec(CB, rev), spec(128, rev)]
        args += [q4, k4, v4, bg4]
    out_sd = jax.ShapeDtypeStruct((B // ns, ns, L, CB), F32)
    of, ob = pl.pallas_call(
        functools.partial(_dn_scan_kernel, ns),
        grid=(B // ns, N),
        in_specs=in_specs,
        out_specs=[spec(CB, False), spec(CB, True)],
        out_shape=[out_sd, out_sd],
        scratch_shapes=[pltpu.VMEM((2 * ns, 2, 256, 256), F32)],
        compiler_params=_cparams(("arbitrary", "arbitrary")),
        name="dn_scan",
    )(*args)
    return of.reshape(T, CB), ob.reshape(T, CB)


def _na_bias_table(rpb):
    qc = jnp.arange(GRID_W)[:, None]
    kc = jnp.arange(GRID_W)[None, :]
    cs = jnp.clip(qc - NA_WIN_C // 2, 0, GRID_W - NA_WIN_C)
    ok = (kc >= cs) & (kc < cs + NA_WIN_C)
    dc = jnp.clip(kc - qc, -(NA_WIN_C - 1), NA_WIN_C - 1) + NA_WIN_C - 1
    delta = jnp.arange(NA_WIN_R)[:, None]
    j = jnp.arange(NA_WIN_R)[None, :]
    dr = j - delta + NA_WIN_R - 1
    t = rpb.astype(F32)[:, dr][:, :, :, dc]
    t = jnp.where(ok[None, None, None], t, NEG_BIG)
    t = t.transpose(1, 0, 3, 2, 4)
    return t.reshape(NA_WIN_R, NA_H, GRID_W, NA_WIN_R * GRID_W)


NA_ROWS_PER_STEP = 2
NA_LOAD_R = NA_WIN_R + NA_ROWS_PER_STEP


def _na_kernel(rows, q_ref, k_ref, v_ref, bias_ref, o_ref):
    nk = NA_WIN_R * GRID_W
    r0 = pl.program_id(1) * NA_ROWS_PER_STEP
    base = jnp.clip(r0 - NA_WIN_R // 2, 0, rows - NA_LOAD_R)
    sls = [slice(h * NA_DH, (h + 1) * NA_DH) for h in range(NA_H)]
    qs, kws, vws, deltas = [], [], [], []
    for a in range(NA_ROWS_PER_STEP):
        r = r0 + a
        rs = jnp.clip(r - NA_WIN_R // 2, 0, rows - NA_WIN_R)
        qs.append(q_ref[a].astype(F32) * (NA_DH ** -0.5))
        kws.append(k_ref[pl.ds(rs - base, NA_WIN_R)].reshape(nk, CB))
        vws.append(v_ref[pl.ds(rs - base, NA_WIN_R)].reshape(nk, CB))
        deltas.append(r - rs)
    ah = [(a, h) for a in range(NA_ROWS_PER_STEP) for h in range(NA_H)]
    s = [_mm_nt(qs[a][:, sls[h]], kws[a][:, sls[h]]) + bias_ref[deltas[a], h] for a, h in ah]
    e = [jnp.exp(x - jnp.max(x, axis=-1, keepdims=True)) for x in s]
    pv = [_mm(x, vws[a][:, sls[h]]) for x, (a, h) in zip(e, ah)]
    outs = [x / jnp.sum(y, axis=-1, keepdims=True) for x, y in zip(pv, e)]
    for a in range(NA_ROWS_PER_STEP):
        o_ref[a] = jnp.concatenate(outs[a * NA_H:(a + 1) * NA_H], axis=1).astype(o_ref.dtype)


def neighborhood_attention(p, bias, B, L):
    rows = L // GRID_W
    nr = NA_ROWS_PER_STEP
    assert rows >= NA_LOAD_R and rows % nr == 0
    p3 = p.reshape(B * rows, GRID_W, P_WIDTH)

    def base(i):
        return jnp.clip(i * nr - NA_WIN_R // 2, 0, rows - NA_LOAD_R)

    win = lambda col: pl.BlockSpec((pl.Element(NA_LOAD_R), pl.Element(GRID_W), pl.Element(CB)),
                                   lambda b, i: (b * rows + base(i), 0, col * CB))
    qo = lambda col: pl.BlockSpec((nr, GRID_W, CB), lambda b, i: (b * (rows // nr) + i, 0, col))
    out = pl.pallas_call(
        functools.partial(_na_kernel, rows),
        grid=(B, rows // nr),
        in_specs=[qo(COL_NQ), win(COL_NK), win(COL_NV),
                  pl.BlockSpec(bias.shape, lambda b, i: (0, 0, 0, 0))],
        out_specs=qo(0),
        out_shape=jax.ShapeDtypeStruct((B * rows, GRID_W, CB), BF16),
        compiler_params=_cparams(("parallel", "arbitrary")),
        name="natten",
    )(p3, p3, p3, bias)
    return out.reshape(B * L, CB)


MERGE_TM = 256


def _merge_kernel(h_ref, ret_ref, of_ref, ob_ref, dg_ref, na_ref, mg0_ref, mg1_ref, mg2_ref,
                  gb_ref, wb_ref, wo_ref, dnw_ref, nf_ref, wr_ref,
                  ho_ref, uo_ref, aff_ref, afft_ref):
    o = of_ref[...] + ob_ref[...]
    dn = []
    for h in range(DN_H):
        sl = slice(h * DN_DV, (h + 1) * DN_DV)
        oh = o[:, sl]
        dn.append(oh * lax.rsqrt(jnp.mean(oh * oh, axis=-1, keepdims=True) + NORM_EPS))
    dn = jnp.concatenate(dn, axis=1) * dnw_ref[...] * _silu(dg_ref[...].astype(F32))
    branches = (ret_ref[...], dn.astype(BF16), na_ref[...])
    merged = None
    for i, (br, mg_ref) in enumerate(zip(branches, (mg0_ref, mg1_ref, mg2_ref))):
        t = (_sigmoid(mg_ref[...].astype(F32) + gb_ref[i:i + 1, :])
             * jnp.dot(br, wb_ref[i], preferred_element_type=F32))
        merged = t if merged is None else merged + t
    h_new = h_ref[...] + jnp.dot(merged.astype(BF16), wo_ref[...], preferred_element_type=F32)
    ho_ref[...] = h_new
    u2 = _rms(h_new, nf_ref[...])
    uo_ref[...] = u2
    logits = _mm3(u2, wr_ref[...])
    lane = lax.broadcasted_iota(jnp.int32, logits.shape, 1)
    logits = jnp.where(lane < EC_EXPERTS, logits, NEG_BIG)
    e = jnp.exp(logits - jnp.max(logits, axis=-1, keepdims=True))
    aff = e / jnp.sum(e, axis=-1, keepdims=True)
    aff_ref[...] = aff.T[0:EC_EXPERTS, :]
    afft_ref[...] = aff


def merge(h, ret, of, ob, na, p, gate_b_l, w_branch_l, w_out_l, dn_norm_w_l, norm_ffn_l, w_router_l):
    T = h.shape[0]
    tm = MERGE_TM
    row = lambda w, col=0: pl.BlockSpec((tm, w), lambda i: (i, col))
    full = lambda shape: pl.BlockSpec(shape, lambda i: (0,) * len(shape))
    wr = jnp.pad(w_router_l.astype(F32), ((0, 0), (0, 128 - EC_EXPERTS)))
    return pl.pallas_call(
        _merge_kernel,
        grid=(T // tm,),
        in_specs=[row(D_MODEL), row(CB), row(CB), row(CB), row(CB, COL_DG), row(CB),
                  row(D_MODEL, 0), row(D_MODEL, 1), row(D_MODEL, 2),
                  full((N_BRANCH, D_MODEL)), full((N_BRANCH, BRANCH_W, D_MODEL)), full((D_MODEL, D_MODEL)),
                  full((1, CB)), full((1, D_MODEL)), full((D_MODEL, 128))],
        out_specs=[row(D_MODEL), row(D_MODEL), pl.BlockSpec((EC_EXPERTS, tm), lambda i: (0, i)), row(128)],
        out_shape=[jax.ShapeDtypeStruct((T, D_MODEL), F32), jax.ShapeDtypeStruct((T, D_MODEL), F32),
                   jax.ShapeDtypeStruct((EC_EXPERTS, T), F32), jax.ShapeDtypeStruct((T, 128), F32)],
        compiler_params=_cparams(("parallel",)),
        name="merge",
    )(h, ret, of, ob, p, na, p, p, p,
      gate_b_l.astype(F32), w_branch_l.astype(BF16), w_out_l.astype(BF16),
      jnp.tile(dn_norm_w_l.astype(F32), DN_H).reshape(1, CB), norm_ffn_l.astype(F32).reshape(1, D_MODEL), wr)


FFN_TS = 256
FFN_GROUPS = 2


def _ffn_kernel(idx_ref, idxn_ref, x_hbm, wg_ref, wu_ref, wd_ref, y_ref, xbuf, sem):
    ts = FFN_TS
    k = pl.program_id(0)
    nk = pl.num_programs(0)
    slot = k % 2

    def row_copy(src_row, s, r):
        return pltpu.make_async_copy(x_hbm.at[pl.ds(src_row, 1), :], xbuf.at[s, pl.ds(r, 1), :], sem.at[s])

    def issue(iref, s):
        def body(r, c):
            row_copy(iref[0, 0, r], s, r).start()
            return c
        lax.fori_loop(0, ts, body, 0, unroll=8)

    def wait_all(s):
        def wbody(r, c):
            row_copy(0, s, r).wait()
            return c
        lax.fori_loop(0, ts, wbody, 0, unroll=8)

    @pl.when(k == 0)
    def _():
        issue(idx_ref, 0)

    wait_all(slot)

    x = xbuf[slot].astype(BF16)
    rows_per_group = ts // FFN_GROUPS
    fc = D_MODEL // FFN_GROUPS
    y = None
    for gi in range(FFN_GROUPS):
        for r in range(gi * rows_per_group, (gi + 1) * rows_per_group):
            row_copy(idxn_ref[0, 0, r], 1 - slot, r).start()
        cs = slice(gi * fc, (gi + 1) * fc)
        hg = jnp.dot(x, wg_ref[0, :, cs], preferred_element_type=F32)
        hu = jnp.dot(x, wu_ref[0, :, cs], preferred_element_type=F32)
        t = jnp.dot((_silu(hg) * hu).astype(BF16), wd_ref[0, cs, :], preferred_element_type=F32)
        y = t if y is None else y + t
    y_ref[...] = y.astype(y_ref.dtype)

    @pl.when(k == nk - 1)
    def _():
        wait_all(1 - slot)


def moe_ffn(idx, x, wg, wu, wd):
    E, cap = idx.shape
    ts = FFN_TS
    nt = cap // ts
    K = E * nt
    idx3 = idx.reshape(K, 1, ts).astype(jnp.int32)
    wspec = pl.BlockSpec((1, D_MODEL, D_MODEL), lambda k: (k // nt, 0, 0))
    return pl.pallas_call(
        _ffn_kernel,
        grid=(K,),
        in_specs=[pl.BlockSpec((1, 1, ts), lambda k: (k, 0, 0), memory_space=pltpu.SMEM),
                  pl.BlockSpec((1, 1, ts), lambda k: (jnp.minimum(k + 1, K - 1), 0, 0), memory_space=pltpu.SMEM),
                  pl.BlockSpec(memory_space=pl.ANY),
                  wspec, wspec, wspec],
        out_specs=pl.BlockSpec((ts, D_MODEL), lambda k: (k, 0)),
        out_shape=jax.ShapeDtypeStruct((E * cap, D_MODEL), BF16),
        scratch_shapes=[pltpu.VMEM((2, ts, D_MODEL), F32), pltpu.SemaphoreType.DMA((2,))],
        compiler_params=_cparams(("arbitrary",)),
        name="moe_ffn",
    )(idx3, idx3, x, wg, wu, wd)


ROUTE_SC = 1024


def _route_kernel(cap, aff_ref, pos_ref, offs_ref, idx_ref):
    R = aff_ref.shape[0]
    a = aff_ref[...]
    keys = pltpu.bitcast(a, jnp.int32)

    def count(mask):
        return jnp.sum(jnp.where(mask, 1.0, 0.0))

    def search(i, thr):
        cand = thr | (jnp.int32(1) << (30 - i))
        return jnp.where(count(keys >= cand) >= cap, cand, thr)

    thr = lax.fori_loop(0, 31, search, jnp.int32(0))
    above = keys > thr
    tie = keys == thr
    need = cap - count(above)

    li = lax.broadcasted_iota(jnp.int32, (128, 128), 0)
    lj = lax.broadcasted_iota(jnp.int32, (128, 128), 1)
    upper = jnp.where(li <= lj, 1.0, 0.0).astype(BF16)
    ri = lax.broadcasted_iota(jnp.int32, (R, R), 0)
    rj = lax.broadcasted_iota(jnp.int32, (R, R), 1)
    lower_strict = jnp.where(rj < ri, 1.0, 0.0).astype(BF16)
    dot = functools.partial(jnp.dot, preferred_element_type=F32)

    def prefix(m):
        pin = dot(m.astype(BF16), upper)
        tot = jnp.broadcast_to(pin[:, 127:128], (R, 128))
        offs = dot(lower_strict, tot.astype(BF16))
        return pin, tot, offs

    tie_f = jnp.where(tie, 1.0, 0.0)
    pin, _, offs = prefix(tie_f)
    sel = above | (tie & ((offs + pin - tie_f) < need))
    sel_f = jnp.where(sel, 1.0, 0.0)
    pin, tot, offs = prefix(sel_f)
    pos_ref[...] = jnp.where(sel, offs + pin - sel_f, -1.0).astype(jnp.int32)
    offs_ref[...] = offs.astype(jnp.int32)

    rinc_row = (offs + tot).T[0:1, :]
    rhs = jnp.concatenate([jnp.ones((R, 128), BF16), tot.astype(BF16)], axis=1)
    pin_b = pin.astype(BF16)
    ones128 = jnp.ones((128, 128), BF16)
    SC = ROUTE_SC
    lane_r = lax.broadcasted_iota(jnp.int32, (SC, R), 1).astype(F32)
    for c in range(cap // SC):
        s_col = (lax.broadcasted_iota(jnp.int32, (SC, R), 0) + c * SC).astype(F32)
        before = jnp.where(rinc_row <= s_col, 1.0, 0.0).astype(BF16)
        rb = dot(before, rhs)
        row, base = rb[:, :128], rb[:, 128:]
        row_r = jnp.concatenate([row] * (R // 128), axis=1)
        onehot = jnp.where(lane_r == row_r, 1.0, 0.0).astype(BF16)
        pin_row = dot(onehot, pin_b)
        k = s_col[:, :128] - base
        lane = dot(jnp.where(pin_row <= k, 1.0, 0.0).astype(BF16), ones128)
        idx = row * 128.0 + lane
        idx_ref[:, c * SC:(c + 1) * SC] = idx.T[0:8, :].astype(jnp.int32)


def route(aff_t, cap):
    E, T = aff_t.shape
    R = T // 128
    assert cap % ROUTE_SC == 0 and R % 128 == 0
    blk = pl.BlockSpec((None, R, 128), lambda e: (e, 0, 0))
    pos, offs, idx = pl.pallas_call(
        functools.partial(_route_kernel, cap),
        grid=(E,),
        in_specs=[blk],
        out_specs=[blk, blk, pl.BlockSpec((None, 8, cap), lambda e: (e, 0, 0))],
        out_shape=[jax.ShapeDtypeStruct((E, R, 128), jnp.int32), jax.ShapeDtypeStruct((E, R, 128), jnp.int32),
                   jax.ShapeDtypeStruct((E, 8, cap), jnp.int32)],
        compiler_params=_cparams(("parallel",)),
        name="route",
    )(aff_t.reshape(E, R, 128))
    return pos.reshape(E, T), offs[:, :, 0], idx[:, 0, :]


CMB_TM = 512
CMB_W = 128
CMB_ALIGN = 16


def _combine_kernel(emit_h, n_rows, start_ref, nwin_ref, h_ref, pos_ref, aff_ref, w_ref, y_hbm, *rest):
    out_refs, (ybuf, yov, acc, sem, sem_ov) = rest[:-5], rest[-5:]
    E, W, tm = EC_EXPERTS, CMB_W, CMB_TM
    i = pl.program_id(0)
    nt = pl.num_programs(0)
    slot = i % 2

    def win_copy(tile, e, s):
        return pltpu.make_async_copy(y_hbm.at[pl.ds(pl.multiple_of(start_ref[tile * E + e], CMB_ALIGN), W), :],
                                     ybuf.at[s, pl.ds(e * W, W), :], sem.at[s])

    def issue(tile, s):
        for e in range(E):
            win_copy(tile, e, s).start()

    @pl.when(i == 0)
    def _():
        issue(0, 0)

    @pl.when(i + 1 < nt)
    def _():
        issue(i + 1, 1 - slot)

    for e in range(E):
        win_copy(i, e, slot).wait()

    pos = pos_ref[...]
    aff = aff_ref[...]
    lane = lax.broadcasted_iota(jnp.int32, (tm, W), 1)

    def expand(base_of, lo_of=None):
        cols = []
        for e in range(E):
            pe = pos[:, e:e + 1]
            hit = pe - base_of(e) == lane
            if lo_of is not None:
                hit = hit & (pe >= lo_of(e))
            cols.append(jnp.where(hit, aff[:, e:e + 1], 0.0).astype(BF16))
        return jnp.concatenate(cols, axis=1)

    acc[...] = jnp.dot(expand(lambda e: start_ref[i * E + e]), ybuf[slot], preferred_element_type=F32)

    for w in range(1, tm // W + 1):
        @pl.when(nwin_ref[i] > w)
        def _():
            def ov_start(e):
                return pl.multiple_of(jnp.minimum(start_ref[i * E + e] + w * W, n_rows - W), CMB_ALIGN)

            def ov_copy(e):
                return pltpu.make_async_copy(y_hbm.at[pl.ds(ov_start(e), W), :], yov.at[pl.ds(e * W, W), :], sem_ov)
            for e in range(E):
                ov_copy(e).start()
            for e in range(E):
                ov_copy(e).wait()
            acc[...] += jnp.dot(expand(ov_start, lambda e: start_ref[i * E + e] + w * W), yov[...],
                                preferred_element_type=F32)

    h_new = h_ref[...] + acc[...]
    if emit_h:
        out_refs[0][...] = h_new
    out_refs[-1][...] = _rms(h_new, w_ref[...]).astype(out_refs[-1].dtype)


def combine(h, y, pos, offs, aff_tok, cap, w, emit_h, norm_dtype):
    T = h.shape[0]
    E, tm, W = EC_EXPERTS, CMB_TM, CMB_W
    nt = T // tm
    n_rows = E * cap
    ebase = (jnp.arange(E, dtype=jnp.int32) * cap)[:, None]
    pos_tok = jnp.where(pos >= 0, pos + ebase, -1).T
    first = offs[:, ::tm // 128] + ebase
    last = jnp.concatenate([first[:, 1:], ebase + cap], axis=1)
    start_al = (first // CMB_ALIGN) * CMB_ALIGN
    nwin = jnp.max((last - start_al + W - 1) // W, axis=0).astype(jnp.int32)
    start = jnp.minimum(start_al, n_rows - W).T.reshape(-1).astype(jnp.int32)
    row = lambda wd: pl.BlockSpec((tm, wd), lambda i, *_: (i, 0))
    outs = ([jax.ShapeDtypeStruct((T, D_MODEL), F32)] if emit_h else []) + [jax.ShapeDtypeStruct((T, D_MODEL), norm_dtype)]
    return pl.pallas_call(
        functools.partial(_combine_kernel, emit_h, n_rows),
        grid_spec=pltpu.PrefetchScalarGridSpec(
            num_scalar_prefetch=2,
            grid=(nt,),
            in_specs=[row(D_MODEL), row(E), row(128), pl.BlockSpec((1, D_MODEL), lambda i, *_: (0, 0)),
                      pl.BlockSpec(memory_space=pl.ANY)],
            out_specs=[row(D_MODEL)] * len(outs),
            scratch_shapes=[pltpu.VMEM((2, E * W, D_MODEL), BF16), pltpu.VMEM((E * W, D_MODEL), BF16),
                            pltpu.VMEM((tm, D_MODEL), F32), pltpu.SemaphoreType.DMA((2,)),
                            pltpu.SemaphoreType.DMA(())]),
        out_shape=outs,
        compiler_params=_cparams(("arbitrary",)),
        name="combine",
    )(start, nwin, h, pos_tok, aff_tok, w.astype(F32).reshape(1, D_MODEL), y)


def _trunk(x, prm):
    B, L, _ = x.shape
    T = B * L
    cap = EC_CAPACITY * T // EC_EXPERTS
    h = x.reshape(T, D_MODEL)
    cosf, sinf = _rope_tables(L)
    u = rmsnorm(h, prm["norm_mix"][0].astype(F32), BF16)
    for l in range(DEPTH):
        p = inproj(u, prm["w_main"][l])
        ret = retention(p, prm["ret_lg"][l], prm["ret_gn_w"][l].astype(F32), cosf, sinf, B, L)
        qn, kn, vs, bg = dn_prep(p, u, prm["w_ba"][l], prm["dn_conv_w"][l], prm["dn_A_log"][l],
                                 prm["dn_dt_bias"][l], L)
        of, ob = dn_scan(qn, kn, vs, bg, B, L)
        na = neighborhood_attention(p, prm["na_bias"][l], B, L)
        h, u_ffn, aff_t, aff_tok = merge(h, ret, of, ob, na, p, prm["gate_b"][l], prm["w_branch"][l],
                                         prm["w_out"][l], prm["dn_norm_w"][l], prm["norm_ffn"][l],
                                         prm["w_router"][l])
        pos, offs, idx = route(aff_t, cap)
        y = moe_ffn(idx, u_ffn, prm["w_gate_e"][l], prm["w_up_e"][l], prm["w_down_e"][l])
        if l + 1 < DEPTH:
            h, u = combine(h, y, pos, offs, aff_tok, cap, prm["norm_mix"][l + 1], True, BF16)
        else:
            (out,) = combine(h, y, pos, offs, aff_tok, cap, prm["final_norm"], False, F32)
    return out.reshape(B, L, D_MODEL)


def kernel(x_prompt, x_sample, norm_mix, w_in, gate_b, ret_decay, ret_gn_w, dn_conv_w, dn_A_log, dn_dt_bias, dn_norm_w, na_rpb, w_branch, w_out, norm_ffn, w_router, w_gate_e, w_up_e, w_down_e, final_norm):
    packed = [pack_w_in(w_in[l]) for l in range(DEPTH)]
    prm = dict(
        norm_mix=norm_mix, gate_b=gate_b, ret_gn_w=ret_gn_w, dn_conv_w=dn_conv_w, dn_A_log=dn_A_log,
        dn_dt_bias=dn_dt_bias, dn_norm_w=dn_norm_w, w_branch=w_branch, w_out=w_out, norm_ffn=norm_ffn,
        w_router=w_router, final_norm=final_norm,
        w_main=[pk[0] for pk in packed], w_ba=[pk[1] for pk in packed],
        ret_lg=jax.nn.log_sigmoid(ret_decay.astype(F32)),
        na_bias=[_na_bias_table(na_rpb[l]) for l in range(DEPTH)],
        w_gate_e=w_gate_e.astype(BF16), w_up_e=w_up_e.astype(BF16), w_down_e=w_down_e.astype(BF16),
    )
    return (_trunk(x_prompt, prm), _trunk(x_sample, prm))
```
